```python
import jax, jax.numpy as jnp
from jax import lax
import numpy as np

D_MODEL = 1024
BATCH = 8
SEQ = 4096
DEPTH = 4

HGRN_KDIM = 128
HGRN_VDIM = 128
HGRN_HEADS = D_MODEL // HGRN_KDIM
HGRN_WIDTH = HGRN_HEADS * HGRN_KDIM
GLA_CHUNK = 64
SG_CHUNK = 128
SG_GROUP_CH = 128
SG_GROUPS = D_MODEL // SG_GROUP_CH
SG_WIDTH = SG_GROUPS * SG_GROUP_CH
IN_SPLITS = (HGRN_WIDTH, HGRN_WIDTH, HGRN_WIDTH, HGRN_WIDTH, HGRN_WIDTH,
             SG_WIDTH, SG_WIDTH, SG_WIDTH, D_MODEL, D_MODEL)
IN_WIDTH = 5 * HGRN_WIDTH + 3 * SG_WIDTH + 2 * D_MODEL
RMS_EPS = 1e-6
LN_EPS = 1e-5
LB_FLOOR = 1e-20

kernel_name = "hgrn2_spatial_gating_hybrid_encoder"


def _rmsnorm(x, w, eps=RMS_EPS):
    xf = x.astype(jnp.float32)
    y = xf * lax.rsqrt(jnp.mean(xf * xf, axis=-1, keepdims=True) + eps)
    return (y * w.astype(jnp.float32)).astype(x.dtype)


def _layernorm(x, w, b, eps=LN_EPS):
    xf = x.astype(jnp.float32)
    mu = jnp.mean(xf, axis=-1, keepdims=True)
    var = jnp.mean(jnp.square(xf - mu), axis=-1, keepdims=True)
    y = (xf - mu) * lax.rsqrt(var + eps)
    return (y * w.astype(jnp.float32) + b.astype(jnp.float32)).astype(x.dtype)


def _gla_chunked(q, k, v, log_f):
    b, h, s, dk = q.shape
    dv = v.shape[-1]
    n = s // GLA_CHUNK

    def to_chunks(t):
        return jnp.moveaxis(t.reshape(b, h, n, GLA_CHUNK, t.shape[-1]), 2, 0)

    qc, kc, vc = to_chunks(q), to_chunks(k), to_chunks(v)
    gc = jnp.cumsum(to_chunks(log_f), axis=-2)
    mask = jnp.tril(jnp.ones((GLA_CHUNK, GLA_CHUNK), dtype=bool))[:, :, None]

    def step(state, xs):
        qb, kb, vb, gb = xs
        diff = gb[..., :, None, :] - gb[..., None, :, :]
        decay = jnp.where(mask, jnp.exp(jnp.where(mask, diff, 0.0)), 0.0)
        attn = jnp.einsum('bhtd,bhsd,bhtsd->bhts', qb, kb, decay)
        o = jnp.einsum('bhts,bhsv->bhtv', attn, vb)
        o = o + jnp.einsum('bhtd,bhdv->bhtv', qb * jnp.exp(gb), state)
        g_last = gb[..., -1:, :]
        k_dec = kb * jnp.exp(g_last - gb)
        state = state * jnp.exp(g_last[..., 0, :])[..., None] + jnp.einsum('bhsd,bhsv->bhdv', k_dec, vb)
        return state, o

    state0 = jnp.zeros((b, h, dk, dv), jnp.float32)
    _, outs = lax.scan(step, state0, (qc, kc, vc, gc))
    return jnp.moveaxis(outs, 0, 2).reshape(b, h, s, dv)


def _hgrn2_forget(f_raw, lb):
    z = f_raw.astype(jnp.float32)
    k = (1.0 - lb) * jax.nn.sigmoid(-z)
    log_f = jnp.logaddexp(jnp.log(jnp.maximum(lb, LB_FLOOR)),
                          jnp.log1p(-lb) + jax.nn.log_sigmoid(z))
    return k, log_f


def _hgrn2_branch(q_raw, ffwd_raw, fbwd_raw, i_raw, g_raw, lb_fwd, lb_bwd, gnorm_w):
    b, s, _ = q_raw.shape

    def heads(t):
        return t.reshape(b, s, HGRN_HEADS, -1).transpose(0, 2, 1, 3).astype(jnp.float32)

    def flip(t):
        return jnp.flip(t, axis=2)

    q = heads(jax.nn.silu(q_raw)) * (HGRN_KDIM ** -0.5)
    v = heads(i_raw)
    k_f, g_f = _hgrn2_forget(ffwd_raw, lb_fwd)
    k_b, g_b = _hgrn2_forget(fbwd_raw, lb_bwd)
    o_fwd = _gla_chunked(q, heads(k_f), v, heads(g_f))
    o_bwd = flip(_gla_chunked(flip(q), flip(heads(k_b)), flip(v), flip(heads(g_b))))
    o = (o_fwd + o_bwd).transpose(0, 2, 1, 3)
    o = _rmsnorm(o, gnorm_w).reshape(b, s, HGRN_HEADS * HGRN_VDIM)
    return (o * jax.nn.silu(g_raw.astype(jnp.float32))).astype(q_raw.dtype)


def _spatial_gating_branch(u_raw, v_raw, g_raw, ln_w, ln_b, w_s, b_s):
    b, s, _ = u_raw.shape
    u = jax.nn.gelu(u_raw)
    v = _layernorm(jax.nn.gelu(v_raw), ln_w, ln_b)
    vc = v.reshape(b, s // SG_CHUNK, SG_CHUNK, SG_GROUPS, SG_GROUP_CH)
    mixed = jnp.einsum('gts,bnsgc->bntgc', w_s, vc) + b_s.T[None, None, :, :, None]
    return u * mixed.reshape(b, s, SG_WIDTH) * jax.nn.silu(g_raw)


def setup_inputs(seed: int = 0) -> dict:
    key = jax.random.key(seed)
    ks = jax.random.split(key, 16)
    f32 = jnp.float32
    x = jax.random.normal(ks[0], (BATCH, SEQ, D_MODEL), f32)
    norm_w = 1.0 + 0.05 * jax.random.normal(ks[1], (DEPTH, D_MODEL), f32)
    w_in = jax.random.normal(ks[2], (DEPTH, D_MODEL, IN_WIDTH), f32) * D_MODEL ** -0.5
    lower_bounds = 0.1 * jax.random.normal(ks[3], (DEPTH, 2, HGRN_WIDTH), f32)
    gnorm_w = 1.0 + 0.05 * jax.random.normal(ks[4], (DEPTH, HGRN_VDIM), f32)
    ln_w = 1.0 + 0.05 * jax.random.normal(ks[5], (DEPTH, SG_WIDTH), f32)
    ln_b = 0.02 * jax.random.normal(ks[6], (DEPTH, SG_WIDTH), f32)
    w_s = jax.random.normal(ks[7], (DEPTH, SG_GROUPS, SG_CHUNK, SG_CHUNK), f32) * SG_CHUNK ** -0.5
    b_s = 1.0 + 0.1 * jax.random.normal(ks[8], (DEPTH, SG_GROUPS, SG_CHUNK), f32)
    w_proj_a = jax.random.normal(ks[9], (DEPTH, HGRN_WIDTH, D_MODEL), f32) * HGRN_WIDTH ** -0.5
    w_proj_b = jax.random.normal(ks[10], (DEPTH, SG_WIDTH, D_MODEL), f32) * SG_WIDTH ** -0.5
    w_out = jax.random.normal(ks[11], (DEPTH, D_MODEL, D_MODEL), f32) * D_MODEL ** -0.5
    final_norm_w = 1.0 + 0.05 * jax.random.normal(ks[12], (D_MODEL,), f32)
    return {"x": x, "norm_w": norm_w, "w_in": w_in, "lower_bounds": lower_bounds,
            "gnorm_w": gnorm_w, "ln_w": ln_w, "ln_b": ln_b, "w_s": w_s, "b_s": b_s,
            "w_proj_a": w_proj_a, "w_proj_b": w_proj_b, "w_out": w_out,
            "final_norm_w": final_norm_w}


def reference(x, norm_w, w_in, lower_bounds, gnorm_w, ln_w, ln_b, w_s, b_s,
              w_proj_a, w_proj_b, w_out, final_norm_w):
    lb = jax.nn.softmax(lower_bounds.astype(jnp.float32), axis=0)
    lb = jnp.cumsum(lb, axis=0) - lb[0]
    split_points = []
    acc = 0
    for size in IN_SPLITS[:-1]:
        acc += size
        split_points.append(acc)
    for layer in range(DEPTH):
        h = _rmsnorm(x, norm_w[layer])
        proj = jnp.einsum('bsd,de->bse', h, w_in[layer])
        (q_raw, ffwd_raw, fbwd_raw, i_raw, ga_raw,
         u_raw, v_raw, gb_raw, ma_raw, mb_raw) = jnp.split(proj, split_points, axis=-1)
        y_a = _hgrn2_branch(q_raw, ffwd_raw, fbwd_raw, i_raw, ga_raw,
                            lb[layer, 0], lb[layer, 1], gnorm_w[layer])
        y_b = _spatial_gating_branch(u_raw, v_raw, gb_raw, ln_w[layer], ln_b[layer],
                                     w_s[layer], b_s[layer])
        merged = (jax.nn.sigmoid(ma_raw) * jnp.einsum('bsw,wd->bsd', y_a, w_proj_a[layer])
                  + jax.nn.sigmoid(mb_raw) * jnp.einsum('bsw,wd->bsd', y_b, w_proj_b[layer]))
        x = x + jnp.einsum('bsd,de->bse', merged, w_out[layer])
    return _rmsnorm(x, final_norm_w)
```

```python
import functools

import jax
import jax.numpy as jnp
from jax import lax
from jax.experimental import pallas as pl
from jax.experimental.pallas import tpu as pltpu

D_MODEL = 1024
DEPTH = 4
HEAD_DIM = 128
N_HEADS = D_MODEL // HEAD_DIM
SG_CHUNK = 128
SG_GROUPS = D_MODEL // HEAD_DIM
IN_WIDTH = 10 * D_MODEL
RMS_EPS = 1e-6
LN_EPS = 1e-5
LB_FLOOR = 1e-20

COL_Q, COL_FF, COL_FB, COL_I, COL_GA, COL_U, COL_V, COL_GB, COL_MA, COL_MB = range(10)

REC_CHUNK = 128
LEVELS = (64, 32, 16, 8, 4, 2)

VMEM_LIMIT_BYTES = 48 * 1024 * 1024

F32 = jnp.float32
BF16 = jnp.bfloat16


def _compiler_params(semantics):
    return pltpu.CompilerParams(dimension_semantics=semantics, vmem_limit_bytes=VMEM_LIMIT_BYTES)


def _lower_bound_kernel(lb_ref, out_ref):
    x = lb_ref[...]
    m = jnp.max(x, axis=0, keepdims=True)
    e = jnp.exp(x - m)
    p = e / jnp.sum(e, axis=0, keepdims=True)
    acc = p[0:1]
    rows = [acc - p[0:1]]
    for layer in range(1, DEPTH):
        acc = acc + p[layer:layer + 1]
        rows.append(acc - p[0:1])
    out_ref[...] = jnp.concatenate(rows, axis=0)


def _lower_bounds(lower_bounds):
    flat = lower_bounds.astype(F32).reshape(DEPTH, 2 * D_MODEL)
    out = pl.pallas_call(
        _lower_bound_kernel,
        out_shape=jax.ShapeDtypeStruct((DEPTH, 2 * D_MODEL), F32),
        name="lower_bounds",
    )(flat)
    return out.reshape(DEPTH, 2, D_MODEL)


def _rmsnorm_kernel(x_ref, w_ref, o_ref):
    x = x_ref[...]
    y = x * lax.rsqrt(jnp.mean(x * x, axis=-1, keepdims=True) + RMS_EPS)
    o_ref[...] = (y * w_ref[...]).astype(o_ref.dtype)


def _rmsnorm(x2d, w, out_dtype, block_rows=1024):
    t = x2d.shape[0]
    return pl.pallas_call(
        _rmsnorm_kernel,
        grid=(t // block_rows,),
        in_specs=[pl.BlockSpec((block_rows, D_MODEL), lambda i: (i, 0)),
                  pl.BlockSpec((1, D_MODEL), lambda i: (0, 0))],
        out_specs=pl.BlockSpec((block_rows, D_MODEL), lambda i: (i, 0)),
        out_shape=jax.ShapeDtypeStruct((t, D_MODEL), out_dtype),
        compiler_params=_compiler_params(("parallel",)),
        name="rmsnorm",
    )(x2d, w.reshape(1, D_MODEL))


def _in_proj_kernel(h_ref, w_ref, o_ref):
    o_ref[...] = jnp.dot(h_ref[...], w_ref[...], preferred_element_type=F32).astype(o_ref.dtype)


def _in_proj(h, w, block_m=2048, block_n=1024):
    t = h.shape[0]
    return pl.pallas_call(
        _in_proj_kernel,
        grid=(IN_WIDTH // block_n, t // block_m),
        in_specs=[pl.BlockSpec((block_m, D_MODEL), lambda j, i: (i, 0)),
                  pl.BlockSpec((D_MODEL, block_n), lambda j, i: (0, j))],
        out_specs=pl.BlockSpec((block_m, block_n), lambda j, i: (i, j)),
        out_shape=jax.ShapeDtypeStruct((t, IN_WIDTH), BF16),
        compiler_params=_compiler_params(("parallel", "parallel")),
        name="in_proj",
    )(h, w)


def _dot_nt(a, b):
    return lax.dot_general(a, b, (((1,), (1,)), ((), ())), preferred_element_type=F32)


def _dot_tn(a, b):
    return lax.dot_general(a, b, (((0,), (0,)), ((), ())), preferred_element_type=F32)


def _cumsum_rows(tri, x):
    hi = x.astype(BF16)
    lo = (x - hi.astype(F32)).astype(BF16)
    r = jnp.dot(tri, jnp.concatenate([hi, lo], axis=1), preferred_element_type=F32)
    return r[:, :HEAD_DIM] + r[:, HEAD_DIM:]


def _forget(z, floor_lb, one_minus_lb):
    s = jax.nn.sigmoid(z)
    f = floor_lb + one_minus_lb * s
    return one_minus_lb * (1.0 - s), f, jnp.log(f)


def _boundary_rows(g, block, row_in_parent):
    c = g.shape[0]
    parent = 2 * block
    if parent >= 8:
        g3 = g.reshape(c // parent, parent, HEAD_DIM)
        r = jnp.broadcast_to(g3[:, row_in_parent:row_in_parent + 1, :], g3.shape)
        return r.reshape(c, HEAD_DIM)
    g3 = g.reshape(c // 8, 8, HEAD_DIM)
    sub = lax.broadcasted_iota(jnp.int32, g3.shape, 1)
    r = jnp.broadcast_to(g3[:, row_in_parent:row_in_parent + 1, :], g3.shape)
    for p in range(1, 8 // parent):
        row = p * parent + row_in_parent
        r = jnp.where(sub >= p * parent, jnp.broadcast_to(g3[:, row:row + 1, :], g3.shape), r)
    return r.reshape(c, HEAD_DIM)


def _gla_kernel(q_ref, ff_ref, fb_ref, v_ref, q2_ref, fb2_ref, v2_ref, lb_ref,
                o_ref, o2_ref, sf_ref, sb_ref):
    c = REC_CHUNK

    @pl.when(pl.program_id(1) == 0)
    def _():
        sf_ref[...] = jnp.zeros_like(sf_ref)
        sb_ref[...] = jnp.zeros_like(sb_ref)

    row = lax.broadcasted_iota(jnp.int32, (c, c), 0)
    col = lax.broadcasted_iota(jnp.int32, (c, c), 1)
    tri_lower = jnp.where(col <= row, 1.0, 0.0).astype(BF16)
    tri_upper = jnp.where(col >= row, 1.0, 0.0).astype(BF16)
    xor = row ^ col
    row_c = lax.broadcasted_iota(jnp.int32, (c, HEAD_DIM), 0)
    scale = HEAD_DIM ** -0.5

    for h in range(N_HEADS):
        hs = slice(h * HEAD_DIM, (h + 1) * HEAD_DIM)
        lb_f = lb_ref[0:1, hs]
        lb_b = lb_ref[1:2, hs]
        floor_f, oml_f = jnp.maximum(lb_f, LB_FLOOR), 1.0 - lb_f
        floor_b, oml_b = jnp.maximum(lb_b, LB_FLOOR), 1.0 - lb_b

        q_raw = q_ref[0, :, hs].astype(F32)
        q = q_raw * jax.nn.sigmoid(q_raw) * scale
        v = v_ref[0, :, hs]
        kf, f_f, logf_f = _forget(ff_ref[0, :, hs].astype(F32), floor_f, oml_f)
        kb, f_b, logf_b = _forget(fb_ref[0, :, hs].astype(F32), floor_b, oml_b)
        gf = _cumsum_rows(tri_lower, logf_f)
        gb = _cumsum_rows(tri_upper, logf_b)

        attn = None
        for block in LEVELS:
            d_f = gf - _boundary_rows(gf, block, block - 1)
            d_b = gb - _boundary_rows(gb, block, block)
            odd = (row_c & block) != 0
            q_l = q * jnp.exp(jnp.minimum(d_f, d_b))
            k_l = jnp.where(odd, kb, kf) * jnp.exp(-jnp.maximum(d_f, d_b))
            p = _dot_nt(q_l.astype(BF16), k_l.astype(BF16))
            attn = p if attn is None else jnp.where(xor < 2 * block, p, attn)
        odd = (row_c & 1) != 0
        p = _dot_nt((q * jnp.where(odd, f_f, f_b)).astype(BF16), jnp.where(odd, kb, kf).astype(BF16))
        attn = jnp.where(xor < 2, p, attn)
        p = _dot_nt(q.astype(BF16), (kf + kb).astype(BF16))
        attn = jnp.where(xor < 1, p, attn)

        s_f = sf_ref[h]
        o = jnp.dot(attn.astype(BF16), v, preferred_element_type=F32)
        o = o + _dot_nt((q * jnp.exp(gf)).astype(BF16), s_f.astype(BF16))
        o_ref[0, :, hs] = o.astype(o_ref.dtype)
        g_last = gf[c - 1:c, :]
        k_dec = kf * jnp.exp(g_last - gf)
        sf_ref[h] = s_f * jnp.exp(g_last) + _dot_tn(v, k_dec.astype(BF16))

        q_raw = q2_ref[0, :, hs].astype(F32)
        q = q_raw * jax.nn.sigmoid(q_raw) * scale
        v = v2_ref[0, :, hs]
        kb, _, logf_b = _forget(fb2_ref[0, :, hs].astype(F32), floor_b, oml_b)
        gb = _cumsum_rows(tri_upper, logf_b)
        s_b = sb_ref[h]
        o2_ref[0, :, hs] = _dot_nt((q * jnp.exp(gb)).astype(BF16), s_b.astype(BF16)).astype(o2_ref.dtype)
        g_first = gb[0:1, :]
        k_dec = kb * jnp.exp(g_first - gb)
        sb_ref[h] = s_b * jnp.exp(g_first) + _dot_tn(v, k_dec.astype(BF16))


def _gla(proj3, lb_layer):
    b, s, _ = proj3.shape
    n = s // REC_CHUNK

    def fwd(colblock):
        return pl.BlockSpec((1, REC_CHUNK, D_MODEL), lambda bi, ni: (bi, ni, colblock))

    def bwd(colblock):
        return pl.BlockSpec((1, REC_CHUNK, D_MODEL), lambda bi, ni: (bi, n - 1 - ni, colblock))

    out_shape = jax.ShapeDtypeStruct((b, s, D_MODEL), F32)
    return pl.pallas_call(
        _gla_kernel,
        grid=(b, n),
        in_specs=[fwd(COL_Q), fwd(COL_FF), fwd(COL_FB), fwd(COL_I),
                  bwd(COL_Q), bwd(COL_FB), bwd(COL_I),
                  pl.BlockSpec((2, D_MODEL), lambda bi, ni: (0, 0))],
        out_specs=[pl.BlockSpec((1, REC_CHUNK, D_MODEL), lambda bi, ni: (bi, ni, 0)),
                   pl.BlockSpec((1, REC_CHUNK, D_MODEL), lambda bi, ni: (bi, n - 1 - ni, 0))],
        out_shape=[out_shape, out_shape],
        scratch_shapes=[pltpu.VMEM((N_HEADS, HEAD_DIM, HEAD_DIM), F32),
                        pltpu.VMEM((N_HEADS, HEAD_DIM, HEAD_DIM), F32)],
        compiler_params=_compiler_params(("parallel", "arbitrary")),
        name="gla",
    )(proj3, proj3, proj3, proj3, proj3, proj3, proj3, lb_layer)


def _merge_kernel(o1_ref, o2_ref, ga_ref, u_ref, v_ref, gb_ref, ma_ref, mb_ref, x_ref,
                  gnw_ref, lnw_ref, lnb_ref, ws_ref, bs_ref, wa_ref, wb_ref, wo_ref, nw_ref,
                  xo_ref, ho_ref):
    rows = x_ref.shape[0]

    o = o1_ref[...] + o2_ref[...]
    ga = ga_ref[...].astype(F32)
    gate_a = ga * jax.nn.sigmoid(ga)
    parts = []
    for h in range(N_HEADS):
        hs = slice(h * HEAD_DIM, (h + 1) * HEAD_DIM)
        oh = o[:, hs]
        oh = oh * lax.rsqrt(jnp.mean(oh * oh, axis=-1, keepdims=True) + RMS_EPS) * gnw_ref[...]
        parts.append((oh * gate_a[:, hs]).astype(BF16))
    y_a = jnp.concatenate(parts, axis=1)

    u = jax.nn.gelu(u_ref[...].astype(F32))
    v = jax.nn.gelu(v_ref[...].astype(F32))
    mu = jnp.mean(v, axis=-1, keepdims=True)
    vc = v - mu
    var = jnp.mean(vc * vc, axis=-1, keepdims=True)
    v = (vc * lax.rsqrt(var + LN_EPS) * lnw_ref[...] + lnb_ref[...]).astype(BF16)
    gb = gb_ref[...].astype(F32)
    gate_b = gb * jax.nn.sigmoid(gb)
    row_parts = []
    for ci in range(rows // SG_CHUNK):
        rs = slice(ci * SG_CHUNK, (ci + 1) * SG_CHUNK)
        parts = []
        for g in range(SG_GROUPS):
            hs = slice(g * HEAD_DIM, (g + 1) * HEAD_DIM)
            mixed = jnp.dot(ws_ref[g], v[rs, hs], preferred_element_type=F32) + bs_ref[g]
            parts.append((u[rs, hs] * mixed * gate_b[rs, hs]).astype(BF16))
        row_parts.append(jnp.concatenate(parts, axis=1))
    y_b = jnp.concatenate(row_parts, axis=0)

    merged = (jax.nn.sigmoid(ma_ref[...].astype(F32)) * jnp.dot(y_a, wa_ref[...], preferred_element_type=F32)
              + jax.nn.sigmoid(mb_ref[...].astype(F32)) * jnp.dot(y_b, wb_ref[...], preferred_element_type=F32))
    x_new = x_ref[...] + jnp.dot(merged.astype(BF16), wo_ref[...], preferred_element_type=F32)
    xo_ref[...] = x_new
    y = x_new * lax.rsqrt(jnp.mean(x_new * x_new, axis=-1, keepdims=True) + RMS_EPS)
    ho_ref[...] = (y * nw_ref[...]).astype(ho_ref.dtype)


def _merge(o1, o2, proj, x, gnorm_w, ln_w, ln_b, w_s, b_s, w_a, w_b, w_o, next_norm_w, h_dtype, block_rows=256):
    t = x.shape[0]

    def tok(colblock=0):
        return pl.BlockSpec((block_rows, D_MODEL), lambda i: (i, colblock))

    def whole(shape):
        return pl.BlockSpec(shape, lambda i: (0,) * len(shape))

    bs_full = jnp.broadcast_to(b_s.astype(F32)[:, :, None], (SG_GROUPS, SG_CHUNK, HEAD_DIM))
    return pl.pallas_call(
        _merge_kernel,
        grid=(t // block_rows,),
        in_specs=[tok(), tok(), tok(COL_GA), tok(COL_U), tok(COL_V), tok(COL_GB), tok(COL_MA), tok(COL_MB), tok(),
                  whole((1, HEAD_DIM)), whole((1, D_MODEL)), whole((1, D_MODEL)),
                  whole((SG_GROUPS, SG_CHUNK, SG_CHUNK)), whole((SG_GROUPS, SG_CHUNK, HEAD_DIM)),
                  whole((D_MODEL, D_MODEL)), whole((D_MODEL, D_MODEL)), whole((D_MODEL, D_MODEL)),
                  whole((1, D_MODEL))],
        out_specs=[tok(), tok()],
        out_shape=[jax.ShapeDtypeStruct((t, D_MODEL), F32), jax.ShapeDtypeStruct((t, D_MODEL), h_dtype)],
        compiler_params=_compiler_params(("parallel",)),
        name="merge",
    )(o1, o2, proj, proj, proj, proj, proj, proj, x,
      gnorm_w.reshape(1, HEAD_DIM), ln_w.reshape(1, D_MODEL), ln_b.reshape(1, D_MODEL),
      w_s.astype(BF16), bs_full, w_a, w_b, w_o, next_norm_w.reshape(1, D_MODEL))


def kernel(x, norm_w, w_in, lower_bounds, gnorm_w, ln_w, ln_b, w_s, b_s, w_proj_a, w_proj_b, w_out, final_norm_w):
    batch, seq, _ = x.shape
    t = batch * seq
    lb = _lower_bounds(lower_bounds)
    x2 = x.reshape(t, D_MODEL)
    h = _rmsnorm(x2, norm_w[0], BF16)
    for layer in range(DEPTH):
        last = layer == DEPTH - 1
        proj = _in_proj(h, w_in[layer].astype(BF16))
        o1, o2 = _gla(proj.reshape(batch, seq, IN_WIDTH), lb[layer])
        x2, h = _merge(o1.reshape(t, D_MODEL), o2.reshape(t, D_MODEL), proj, x2,
                       gnorm_w[layer], ln_w[layer], ln_b[layer], w_s[layer], b_s[layer],
                       w_proj_a[layer].astype(BF16), w_proj_b[layer].astype(BF16), w_out[layer].astype(BF16),
                       final_norm_w if last else norm_w[layer + 1], F32 if last else BF16)
    return h.reshape(batch, seq, D_MODEL)
```

```python
import functools

import jax
import jax.numpy as jnp
from jax import lax
from jax.experimental import pallas as pl
from jax.experimental.pallas import tpu as pltpu

D_MODEL = 1024
DEPTH = 4
HEAD_DIM = 128
N_HEADS = D_MODEL // HEAD_DIM
SG_CHUNK = 128
SG_GROUPS = D_MODEL // HEAD_DIM
IN_WIDTH = 10 * D_MODEL
RMS_EPS = 1e-6
LN_EPS = 1e-5
LB_FLOOR = 1e-20

COL_Q, COL_FF, COL_FB, COL_I, COL_GA, COL_U, COL_V, COL_GB, COL_MA, COL_MB = range(10)

REC_CHUNK = 128
LEVELS = (64, 32, 16, 8, 4, 2)

VMEM_LIMIT_BYTES = 48 * 1024 * 1024

F32 = jnp.float32
BF16 = jnp.bfloat16


def _compiler_params(semantics):
    return pltpu.CompilerParams(dimension_semantics=semantics, vmem_limit_bytes=VMEM_LIMIT_BYTES)


def _lower_bound_kernel(lb_ref, out_ref):
    x = lb_ref[...]
    m = jnp.max(x, axis=0, keepdims=True)
    e = jnp.exp(x - m)
    p = e / jnp.sum(e, axis=0, keepdims=True)
    acc = p[0:1]
    rows = [acc - p[0:1]]
    for layer in range(1, DEPTH):
        acc = acc + p[layer:layer + 1]
        rows.append(acc - p[0:1])
    out_ref[...] = jnp.concatenate(rows, axis=0)


def _lower_bounds(lower_bounds):
    flat = lower_bounds.astype(F32).reshape(DEPTH, 2 * D_MODEL)
    out = pl.pallas_call(
        _lower_bound_kernel,
        out_shape=jax.ShapeDtypeStruct((DEPTH, 2 * D_MODEL), F32),
        name="lower_bounds",
    )(flat)
    return out.reshape(DEPTH, 2, D_MODEL)


def _rmsnorm_kernel(x_ref, w_ref, o_ref):
    x = x_ref[...]
    y = x * lax.rsqrt(jnp.mean(x * x, axis=-1, keepdims=True) + RMS_EPS)
    o_ref[...] = (y * w_ref[...]).astype(o_ref.dtype)


def _rmsnorm(x2d, w, out_dtype, block_rows=1024):
    t = x2d.shape[0]
    return pl.pallas_call(
        _rmsnorm_kernel,
        grid=(t // block_rows,),
        in_specs=[pl.BlockSpec((block_rows, D_MODEL), lambda i: (i, 0)),
                  pl.BlockSpec((1, D_MODEL), lambda i: (0, 0))],
        out_specs=pl.BlockSpec((block_rows, D_MODEL), lambda i: (i, 0)),
        out_shape=jax.ShapeDtypeStruct((t, D_MODEL), out_dtype),
        compiler_params=_compiler_params(("parallel",)),
        name="rmsnorm",
    )(x2d, w.reshape(1, D_MODEL))


def _in_proj_kernel(h_ref, w_ref, o_ref):
    o_ref[...] = jnp.dot(h_ref[...], w_ref[...], preferred_element_type=F32).astype(o_ref.dtype)


def _in_proj(h, w, block_m=2048, block_n=1024):
    t = h.shape[0]
    return pl.pallas_call(
        _in_proj_kernel,
        grid=(IN_WIDTH // block_n, t // block_m),
        in_specs=[pl.BlockSpec((block_m, D_MODEL), lambda j, i: (i, 0)),
                  pl.BlockSpec((D_MODEL, block_n), lambda j, i: (0, j))],
        out_specs=pl.BlockSpec((block_m, block_n), lambda j, i: (i, j)),
        out_shape=jax.ShapeDtypeStruct((t, IN_WIDTH), BF16),
        compiler_params=_compiler_params(("parallel", "parallel")),
        name="in_proj",
    )(h, w)


def _dot_nt(a, b):
    return lax.dot_general(a, b, (((1,), (1,)), ((), ())), preferred_element_type=F32)


def _dot_tn(a, b):
    return lax.dot_general(a, b, (((0,), (0,)), ((), ())), preferred_element_type=F32)


def _cumsum_rows(tri, x):
    width = x.shape[1]
    hi = x.astype(BF16)
    lo = (x - hi.astype(F32)).astype(BF16)
    r = jnp.dot(tri, jnp.concatenate([hi, lo], axis=1), preferred_element_type=F32)
    return r[:, :width] + r[:, width:]


def _forget(z, floor_lb, one_minus_lb):
    s = jax.nn.sigmoid(z)
    f = floor_lb + one_minus_lb * s
    return one_minus_lb * (1.0 - s), f, jnp.log2(f)


def _boundary_rows(g, block, row_in_parent):
    c, width = g.shape
    parent = 2 * block
    g3 = g.reshape(c // 8, 8, width)
    sub = lax.broadcasted_iota(jnp.int32, g3.shape, 1)
    r = jnp.broadcast_to(g3[:, row_in_parent:row_in_parent + 1, :], g3.shape)
    for p in range(1, 8 // parent):
        row = p * parent + row_in_parent
        r = jnp.where(sub >= p * parent, jnp.broadcast_to(g3[:, row:row + 1, :], g3.shape), r)
    return r.reshape(c, width)


def _level_operands(q, kf, kb, gf, gb, block):
    c = q.shape[0]
    if block >= 8:
        q_parts, k_parts = [], []
        for blk in range(c // block):
            rows = slice(blk * block, (blk + 1) * block)
            edge = (blk | 1) * block
            if blk % 2:
                e_q = gf[rows] - gf[edge - 1:edge]
                e_k = gb[edge:edge + 1] - gb[rows]
                k_src = kb
            else:
                e_q = gb[rows] - gb[edge:edge + 1]
                e_k = gf[edge - 1:edge] - gf[rows]
                k_src = kf
            q_parts.append(q[rows] * jnp.exp2(e_q))
            k_parts.append(k_src[rows] * jnp.exp2(e_k))
        q_l = jnp.concatenate(q_parts, axis=0)
        k_l = jnp.concatenate(k_parts, axis=0)
    else:
        d_f = gf - _boundary_rows(gf, block, block - 1)
        d_b = gb - _boundary_rows(gb, block, block)
        odd = (lax.broadcasted_iota(jnp.int32, q.shape, 0) & block) != 0
        q_l = q * jnp.exp2(jnp.minimum(d_f, d_b))
        k_l = jnp.where(odd, kb, kf) * jnp.exp2(-jnp.maximum(d_f, d_b))
    return q_l.astype(BF16), k_l.astype(BF16)


def _gla_kernel(q_ref, ff_ref, fb_ref, v_ref, q2_ref, fb2_ref, v2_ref, lb_ref,
                o_ref, o2_ref, sf_ref, sb_ref):
    c = REC_CHUNK

    @pl.when(pl.program_id(1) == 0)
    def _():
        sf_ref[...] = jnp.zeros_like(sf_ref)
        sb_ref[...] = jnp.zeros_like(sb_ref)

    row = lax.broadcasted_iota(jnp.int32, (c, c), 0)
    col = lax.broadcasted_iota(jnp.int32, (c, c), 1)
    tri_lower = jnp.where(col <= row, 1.0, 0.0).astype(BF16)
    tri_upper = jnp.where(col >= row, 1.0, 0.0).astype(BF16)
    xor = row ^ col
    scale = HEAD_DIM ** -0.5
    heads = [slice(h * HEAD_DIM, (h + 1) * HEAD_DIM) for h in range(N_HEADS)]

    lb_f = lb_ref[0:1, :]
    lb_b = lb_ref[1:2, :]
    floor_f, oml_f = jnp.maximum(lb_f, LB_FLOOR), 1.0 - lb_f
    floor_b, oml_b = jnp.maximum(lb_b, LB_FLOOR), 1.0 - lb_b

    q_raw = q2_ref[0].astype(F32)
    q2 = q_raw * jax.nn.sigmoid(q_raw) * scale
    kb2, _, l2f = _forget(fb2_ref[0].astype(F32), floor_b, oml_b)
    gb2 = _cumsum_rows(tri_upper, l2f)
    g_first = gb2[0:1, :]
    qdec2 = (q2 * jnp.exp2(gb2)).astype(BF16)
    kdec2 = (kb2 * jnp.exp2(g_first - gb2)).astype(BF16)
    decay2 = jnp.exp2(g_first)
    for h, hs in enumerate(heads):
        s_b = sb_ref[h]
        o2_ref[0, :, hs] = _dot_nt(qdec2[:, hs], s_b.astype(BF16)).astype(o2_ref.dtype)
        sb_ref[h] = s_b * decay2[:, hs] + _dot_tn(v2_ref[0, :, hs], kdec2[:, hs])

    q_raw = q_ref[0].astype(F32)
    q = q_raw * jax.nn.sigmoid(q_raw) * scale
    kf, f_f, l2f_f = _forget(ff_ref[0].astype(F32), floor_f, oml_f)
    kb, f_b, l2f_b = _forget(fb_ref[0].astype(F32), floor_b, oml_b)
    gf = _cumsum_rows(tri_lower, l2f_f)
    gb = _cumsum_rows(tri_upper, l2f_b)

    attn = [None] * N_HEADS
    for block in LEVELS:
        q_l, k_l = _level_operands(q, kf, kb, gf, gb, block)
        for h, hs in enumerate(heads):
            p = _dot_nt(q_l[:, hs], k_l[:, hs])
            attn[h] = p if attn[h] is None else jnp.where(xor < 2 * block, p, attn[h])
    odd = (lax.broadcasted_iota(jnp.int32, q.shape, 0) & 1) != 0
    q_l = (q * jnp.where(odd, f_f, f_b)).astype(BF16)
    k_l = jnp.where(odd, kb, kf).astype(BF16)
    q_d = q.astype(BF16)
    k_d = (kf + kb).astype(BF16)
    for h, hs in enumerate(heads):
        attn[h] = jnp.where(xor < 2, _dot_nt(q_l[:, hs], k_l[:, hs]), attn[h])
        attn[h] = jnp.where(xor < 1, _dot_nt(q_d[:, hs], k_d[:, hs]), attn[h])

    g_last = gf[c - 1:c, :]
    qdec = (q * jnp.exp2(gf)).astype(BF16)
    kdec = (kf * jnp.exp2(g_last - gf)).astype(BF16)
    decay = jnp.exp2(g_last)
    for h, hs in enumerate(heads):
        s_f = sf_ref[h]
        v = v_ref[0, :, hs]
        o = jnp.dot(attn[h].astype(BF16), v, preferred_element_type=F32)
        o_ref[0, :, hs] = (o + _dot_nt(qdec[:, hs], s_f.astype(BF16))).astype(o_ref.dtype)
        sf_ref[h] = s_f * decay[:, hs] + _dot_tn(v, kdec[:, hs])


def _gla(proj3, lb_layer):
    b, s, _ = proj3.shape
    n = s // REC_CHUNK

    def fwd(colblock):
        return pl.BlockSpec((1, REC_CHUNK, D_MODEL), lambda bi, ni: (bi, ni, colblock))

    def bwd(colblock):
        return pl.BlockSpec((1, REC_CHUNK, D_MODEL), lambda bi, ni: (bi, n - 1 - ni, colblock))

    out_shape = jax.ShapeDtypeStruct((b, s, D_MODEL), F32)
    return pl.pallas_call(
        _gla_kernel,
        grid=(b, n),
        in_specs=[fwd(COL_Q), fwd(COL_FF), fwd(COL_FB), fwd(COL_I),
                  bwd(COL_Q), bwd(COL_FB), bwd(COL_I),
                  pl.BlockSpec((2, D_MODEL), lambda bi, ni: (0, 0))],
        out_specs=[pl.BlockSpec((1, REC_CHUNK, D_MODEL), lambda bi, ni: (bi, ni, 0)),
                   pl.BlockSpec((1, REC_CHUNK, D_MODEL), lambda bi, ni: (bi, n - 1 - ni, 0))],
        out_shape=[out_shape, out_shape],
        scratch_shapes=[pltpu.VMEM((N_HEADS, HEAD_DIM, HEAD_DIM), F32),
                        pltpu.VMEM((N_HEADS, HEAD_DIM, HEAD_DIM), F32)],
        compiler_params=_compiler_params(("parallel", "arbitrary")),
        name="gla",
    )(proj3, proj3, proj3, proj3, proj3, proj3, proj3, lb_layer)


def _merge_kernel(o1_ref, o2_ref, ga_ref, u_ref, v_ref, gb_ref, ma_ref, mb_ref, x_ref,
                  gnw_ref, lnw_ref, lnb_ref, ws_ref, bs_ref, wa_ref, wb_ref, wo_ref, nw_ref,
                  xo_ref, ho_ref):
    rows = x_ref.shape[0]

    o = o1_ref[...] + o2_ref[...]
    ga = ga_ref[...].astype(F32)
    gate_a = ga * jax.nn.sigmoid(ga)
    parts = []
    for h in range(N_HEADS):
        hs = slice(h * HEAD_DIM, (h + 1) * HEAD_DIM)
        oh = o[:, hs]
        oh = oh * lax.rsqrt(jnp.mean(oh * oh, axis=-1, keepdims=True) + RMS_EPS) * gnw_ref[...]
        parts.append((oh * gate_a[:, hs]).astype(BF16))
    y_a = jnp.concatenate(parts, axis=1)

    u = jax.nn.gelu(u_ref[...].astype(F32))
    v = jax.nn.gelu(v_ref[...].astype(F32))
    mu = jnp.mean(v, axis=-1, keepdims=True)
    vc = v - mu
    var = jnp.mean(vc * vc, axis=-1, keepdims=True)
    v = (vc * lax.rsqrt(var + LN_EPS) * lnw_ref[...] + lnb_ref[...]).astype(BF16)
    gb = gb_ref[...].astype(F32)
    gate_b = gb * jax.nn.sigmoid(gb)
    row_parts = []
    for ci in range(rows // SG_CHUNK):
        rs = slice(ci * SG_CHUNK, (ci + 1) * SG_CHUNK)
        parts = []
        for g in range(SG_GROUPS):
            hs = slice(g * HEAD_DIM, (g + 1) * HEAD_DIM)
            mixed = jnp.dot(ws_ref[g], v[rs, hs], preferred_element_type=F32) + bs_ref[g]
            parts.append((u[rs, hs] * mixed * gate_b[rs, hs]).astype(BF16))
        row_parts.append(jnp.concatenate(parts, axis=1))
    y_b = jnp.concatenate(row_parts, axis=0)

    merged = (jax.nn.sigmoid(ma_ref[...].astype(F32)) * jnp.dot(y_a, wa_ref[...], preferred_element_type=F32)
              + jax.nn.sigmoid(mb_ref[...].astype(F32)) * jnp.dot(y_b, wb_ref[...], preferred_element_type=F32))
    x_new = x_ref[...] + jnp.dot(merged.astype(BF16), wo_ref[...], preferred_element_type=F32)
    xo_ref[...] = x_new
    y = x_new * lax.rsqrt(jnp.mean(x_new * x_new, axis=-1, keepdims=True) + RMS_EPS)
    ho_ref[...] = (y * nw_ref[...]).astype(ho_ref.dtype)


def _merge(o1, o2, proj, x, gnorm_w, ln_w, ln_b, w_s, b_s, w_a, w_b, w_o, next_norm_w, h_dtype, block_rows=256):
    t = x.shape[0]

    def tok(colblock=0):
        return pl.BlockSpec((block_rows, D_MODEL), lambda i: (i, colblock))

    def whole(shape):
        return pl.BlockSpec(shape, lambda i: (0,) * len(shape))

    bs_full = jnp.broadcast_to(b_s.astype(F32)[:, :, None], (SG_GROUPS, SG_CHUNK, HEAD_DIM))
    return pl.pallas_call(
        _merge_kernel,
        grid=(t // block_rows,),
        in_specs=[tok(), tok(), tok(COL_GA), tok(COL_U), tok(COL_V), tok(COL_GB), tok(COL_MA), tok(COL_MB), tok(),
                  whole((1, HEAD_DIM)), whole((1, D_MODEL)), whole((1, D_MODEL)),
                  whole((SG_GROUPS, SG_CHUNK, SG_CHUNK)), whole((SG_GROUPS, SG_CHUNK, HEAD_DIM)),
                  whole((D_MODEL, D_MODEL)), whole((D_MODEL, D_MODEL)), whole((D_MODEL, D_MODEL)),
                  whole((1, D_MODEL))],
        out_specs=[tok(), tok()],
        out_shape=[jax.ShapeDtypeStruct((t, D_MODEL), F32), jax.ShapeDtypeStruct((t, D_MODEL), h_dtype)],
        compiler_params=_compiler_params(("parallel",)),
        name="merge",
    )(o1, o2, proj, proj, proj, proj, proj, proj, x,
      gnorm_w.reshape(1, HEAD_DIM), ln_w.reshape(1, D_MODEL), ln_b.reshape(1, D_MODEL),
      w_s.astype(BF16), bs_full, w_a, w_b, w_o, next_norm_w.reshape(1, D_MODEL))


def kernel(x, norm_w, w_in, lower_bounds, gnorm_w, ln_w, ln_b, w_s, b_s, w_proj_a, w_proj_b, w_out, final_norm_w):
    batch, seq, _ = x.shape
    t = batch * seq
    lb = _lower_bounds(lower_bounds)
    x2 = x.reshape(t, D_MODEL)
    h = _rmsnorm(x2, norm_w[0], BF16)
    for layer in range(DEPTH):
        last = layer == DEPTH - 1
        proj = _in_proj(h, w_in[layer].astype(BF16))
        o1, o2 = _gla(proj.reshape(batch, seq, IN_WIDTH), lb[layer])
        x2, h = _merge(o1.reshape(t, D_MODEL), o2.reshape(t, D_MODEL), proj, x2,
                       gnorm_w[layer], ln_w[layer], ln_b[layer], w_s[layer], b_s[layer],
                       w_proj_a[layer].astype(BF16), w_proj_b[layer].astype(BF16), w_out[layer].astype(BF16),
                       final_norm_w if last else norm_w[layer + 1], F32 if last else BF16)
    return h.reshape(batch, seq, D_MODEL)
```

```python
import math

import jax
import jax.numpy as jnp
from jax import lax
from jax.experimental import pallas as pl
from jax.experimental.pallas import tpu as pltpu

D_MODEL = 1024
DEPTH = 4
HEAD_DIM = 128
N_HEADS = D_MODEL // HEAD_DIM
SG_CHUNK = 128
SG_GROUPS = D_MODEL // HEAD_DIM
RMS_EPS = 1e-6
LN_EPS = 1e-5
LB_FLOOR = 1e-20

COL_Q, COL_FF, COL_FB, COL_I, COL_GA, COL_U, COL_V, COL_GB, COL_MA, COL_MB = range(10)

REC_CHUNK = 128
LEVELS = (64, 32, 16, 8, 4, 2)

PROJ_BLOCK_ROWS = 2048
LOG2E = math.log2(math.e)
VMEM_LIMIT_BYTES = 56 * 1024 * 1024

F32 = jnp.float32
BF16 = jnp.bfloat16


def _compiler_params(semantics):
    return pltpu.CompilerParams(dimension_semantics=semantics, vmem_limit_bytes=VMEM_LIMIT_BYTES)


def _lower_bound_kernel(lb_ref, out_ref):
    x = lb_ref[...]
    m = jnp.max(x, axis=0, keepdims=True)
    e = jnp.exp(x - m)
    p = e / jnp.sum(e, axis=0, keepdims=True)
    acc = p[0:1]
    rows = [acc - p[0:1]]
    for layer in range(1, DEPTH):
        acc = acc + p[layer:layer + 1]
        rows.append(acc - p[0:1])
    out_ref[...] = jnp.concatenate(rows, axis=0)


def _lower_bounds(lower_bounds):
    flat = lower_bounds.astype(F32).reshape(DEPTH, 2 * D_MODEL)
    out = pl.pallas_call(
        _lower_bound_kernel,
        out_shape=jax.ShapeDtypeStruct((DEPTH, 2 * D_MODEL), F32),
        name="lower_bounds",
    )(flat)
    return out.reshape(DEPTH, 2, 1, D_MODEL)


def _rmsnorm_kernel(x_ref, w_ref, o_ref):
    x = x_ref[...]
    y = x * lax.rsqrt(jnp.mean(x * x, axis=-1, keepdims=True) + RMS_EPS)
    o_ref[...] = (y * w_ref[...]).astype(o_ref.dtype)


def _rmsnorm(x2d, w, out_dtype, block_rows=1024):
    t = x2d.shape[0]
    return pl.pallas_call(
        _rmsnorm_kernel,
        grid=(t // block_rows,),
        in_specs=[pl.BlockSpec((block_rows, D_MODEL), lambda i: (i, 0)),
                  pl.BlockSpec((1, D_MODEL), lambda i: (0, 0))],
        out_specs=pl.BlockSpec((block_rows, D_MODEL), lambda i: (i, 0)),
        out_shape=jax.ShapeDtypeStruct((t, D_MODEL), out_dtype),
        compiler_params=_compiler_params(("parallel",)),
        name="rmsnorm",
    )(x2d, w.reshape(1, D_MODEL))


def _sigmoid(x):
    return 1.0 / (1.0 + jnp.exp2(x * (-LOG2E)))


def _gelu(x):
    a = -2.0 * LOG2E * math.sqrt(2.0 / math.pi)
    t = (x * x) * (a * 0.044715) + a
    return x / (1.0 + jnp.exp2(x * t))


def _project_rows(h_ref, w_ref, emit, sub_rows):
    blocks = [slice(r, r + sub_rows) for r in range(0, h_ref.shape[0], sub_rows)]
    acc = jnp.dot(h_ref[blocks[0], :], w_ref[...], preferred_element_type=F32)
    for prev, rows in zip(blocks[:-1], blocks[1:]):
        nxt = jnp.dot(h_ref[rows, :], w_ref[...], preferred_element_type=F32)
        emit(prev, acc)
        acc = nxt
    emit(blocks[-1], acc)


def _proj_plain_kernel(h_ref, w_ref, o_ref):
    def emit(rows, acc):
        o_ref[rows, :] = acc.astype(o_ref.dtype)
    _project_rows(h_ref, w_ref, emit, PROJ_BLOCK_ROWS)


def _proj_silu_kernel(h_ref, w_ref, scale_ref, o_ref):
    def emit(rows, acc):
        o_ref[rows, :] = (acc * _sigmoid(acc) * scale_ref[0]).astype(o_ref.dtype)
    _project_rows(h_ref, w_ref, emit, 256)


def _proj_sigmoid_kernel(h_ref, w_ref, o_ref):
    def emit(rows, acc):
        o_ref[rows, :] = _sigmoid(acc).astype(o_ref.dtype)
    _project_rows(h_ref, w_ref, emit, 128)


def _proj_gelu_kernel(h_ref, w_ref, o_ref):
    def emit(rows, acc):
        o_ref[rows, :] = _gelu(acc).astype(o_ref.dtype)
    _project_rows(h_ref, w_ref, emit, 128)


def _proj_gelu_ln_kernel(h_ref, w_ref, lnw_ref, lnb_ref, o_ref):
    def emit(rows, acc):
        v = _gelu(acc)
        vc = v - jnp.mean(v, axis=-1, keepdims=True)
        var = jnp.mean(vc * vc, axis=-1, keepdims=True)
        o_ref[rows, :] = (vc * lax.rsqrt(var + LN_EPS) * lnw_ref[...] + lnb_ref[...]).astype(o_ref.dtype)
    _project_rows(h_ref, w_ref, emit, 256)


def _proj_forget_kernel(h_ref, w_ref, lb_ref, k_ref, hi_ref, lo_ref):
    lb = lb_ref[0]
    floor_lb, one_minus_lb = jnp.maximum(lb, LB_FLOOR), 1.0 - lb

    def emit(rows, acc):
        gated = one_minus_lb * _sigmoid(acc)
        k_ref[rows, :] = (one_minus_lb - gated).astype(k_ref.dtype)
        l2f = jnp.log2(floor_lb + gated)
        hi = l2f.astype(BF16)
        hi_ref[rows, :] = hi
        lo_ref[rows, :] = (l2f - hi.astype(F32)).astype(BF16)
    _project_rows(h_ref, w_ref, emit, 128)


def _proj(kernel_fn, name, h, w, col_of, n_cols, n_out=1, extra=()):
    t = h.shape[0]
    out_spec = pl.BlockSpec((PROJ_BLOCK_ROWS, D_MODEL), lambda j, i: (i, j))
    out_shape = jax.ShapeDtypeStruct((t, n_cols * D_MODEL), BF16)
    res = pl.pallas_call(
        kernel_fn,
        grid=(n_cols, t // PROJ_BLOCK_ROWS),
        in_specs=[pl.BlockSpec((PROJ_BLOCK_ROWS, D_MODEL), lambda j, i: (i, 0)),
                  pl.BlockSpec((D_MODEL, D_MODEL), lambda j, i: (0, col_of(j)))]
                 + [pl.BlockSpec(shape, imap) for _, shape, imap in extra],
        out_specs=[out_spec] * n_out,
        out_shape=[out_shape] * n_out,
        compiler_params=_compiler_params(("parallel", "parallel")),
        name=name,
    )(h, w, *[a for a, _, _ in extra])
    return res if n_out > 1 else res[0]


def _in_proj(h, w, lb_layer, ln_w, ln_b):
    row = lambda a: a.reshape(1, D_MODEL)
    const = lambda j, i: (0, 0)
    scales = jnp.ones((3, 1, D_MODEL), F32).at[0].multiply(HEAD_DIM ** -0.5)
    gates = _proj(_proj_silu_kernel, "proj_silu", h, w, lambda j: (j * 7 + 1) // 2, 3,
                  extra=[(scales, (1, 1, D_MODEL), lambda j, i: (j, 0, 0))])
    k, hi, lo = _proj(_proj_forget_kernel, "proj_forget", h, w, lambda j: j + COL_FF, 2, n_out=3,
                      extra=[(lb_layer, (1, 1, D_MODEL), lambda j, i: (j, 0, 0))])
    val = _proj(_proj_plain_kernel, "proj_plain", h, w, lambda j: j + COL_I, 1)
    u = _proj(_proj_gelu_kernel, "proj_gelu", h, w, lambda j: j + COL_U, 1)
    v = _proj(_proj_gelu_ln_kernel, "proj_gelu_ln", h, w, lambda j: j + COL_V, 1,
              extra=[(row(ln_w), (1, D_MODEL), const), (row(ln_b), (1, D_MODEL), const)])
    merges = _proj(_proj_sigmoid_kernel, "proj_sigmoid", h, w, lambda j: j + COL_MA, 2)
    return gates, k, hi, lo, val, u, v, merges


def _dot_nt(a, b):
    return lax.dot_general(a, b, (((1,), (1,)), ((), ())), preferred_element_type=F32)


def _dot_tn(a, b):
    return lax.dot_general(a, b, (((0,), (0,)), ((), ())), preferred_element_type=F32)


def _cumsum_rows(tri, hi, lo):
    width = hi.shape[1]
    r = jnp.dot(tri, jnp.concatenate([hi, lo], axis=1), preferred_element_type=F32)
    return r[:, :width] + r[:, width:]


def _boundary_rows(g, block, row_in_parent):
    c, width = g.shape
    parent = 2 * block
    g3 = g.reshape(c // 8, 8, width)
    sub = lax.broadcasted_iota(jnp.int32, g3.shape, 1)
    r = jnp.broadcast_to(g3[:, row_in_parent:row_in_parent + 1, :], g3.shape)
    for p in range(1, 8 // parent):
        row = p * parent + row_in_parent
        r = jnp.where(sub >= p * parent, jnp.broadcast_to(g3[:, row:row + 1, :], g3.shape), r)
    return r.reshape(c, width)


def _level_operands(q, kf, kb, gf, gb, block):
    c = q.shape[0]
    if block >= 8:
        q_parts, k_parts = [], []
        for blk in range(c // block):
            rows = slice(blk * block, (blk + 1) * block)
            edge = (blk | 1) * block
            if blk % 2:
                e_q = gf[rows] - gf[edge - 1:edge]
                e_k = gb[edge:edge + 1] - gb[rows]
                k_src = kb
            else:
                e_q = gb[rows] - gb[edge:edge + 1]
                e_k = gf[edge - 1:edge] - gf[rows]
                k_src = kf
            q_parts.append(q[rows] * jnp.exp2(e_q))
            k_parts.append(k_src[rows] * jnp.exp2(e_k))
        q_l = jnp.concatenate(q_parts, axis=0)
        k_l = jnp.concatenate(k_parts, axis=0)
    else:
        d_f = gf - _boundary_rows(gf, block, block - 1)
        d_b = gb - _boundary_rows(gb, block, block)
        odd = (lax.broadcasted_iota(jnp.int32, q.shape, 0) & block) != 0
        q_l = q * jnp.exp2(jnp.minimum(d_f, d_b))
        k_l = jnp.where(odd, kb, kf) * jnp.exp2(-jnp.maximum(d_f, d_b))
    return q_l.astype(BF16), k_l.astype(BF16)


def _gla_kernel(q_ref, kf_ref, kb_ref, hif_ref, hib_ref, lof_ref, lob_ref, v_ref,
                q2_ref, kb2_ref, hib2_ref, lob2_ref, v2_ref,
                o_ref, o2_ref, sf_ref, sb_ref):
    c = REC_CHUNK

    @pl.when(pl.program_id(1) == 0)
    def _():
        sf_ref[...] = jnp.zeros_like(sf_ref)
        sb_ref[...] = jnp.zeros_like(sb_ref)

    row = lax.broadcasted_iota(jnp.int32, (c, c), 0)
    col = lax.broadcasted_iota(jnp.int32, (c, c), 1)
    tri_lower = jnp.where(col <= row, 1.0, 0.0).astype(BF16)
    tri_upper = jnp.where(col >= row, 1.0, 0.0).astype(BF16)
    xor = row ^ col
    heads = [slice(h * HEAD_DIM, (h + 1) * HEAD_DIM) for h in range(N_HEADS)]

    gb2 = _cumsum_rows(tri_upper, hib2_ref[0], lob2_ref[0])
    g_first = gb2[0:1, :]
    qdec2 = (q2_ref[0].astype(F32) * jnp.exp2(gb2)).astype(BF16)
    kdec2 = (kb2_ref[0].astype(F32) * jnp.exp2(g_first - gb2)).astype(BF16)
    decay2 = jnp.exp2(g_first)
    for h, hs in enumerate(heads):
        s_b = sb_ref[h]
        o2_ref[0, :, hs] = _dot_nt(qdec2[:, hs], s_b.astype(BF16)).astype(o2_ref.dtype)
        sb_ref[h] = s_b * decay2[:, hs] + _dot_tn(v2_ref[0, :, hs], kdec2[:, hs])

    q = q_ref[0].astype(F32)
    kf = kf_ref[0].astype(F32)
    kb = kb_ref[0].astype(F32)
    gf = _cumsum_rows(tri_lower, hif_ref[0], lof_ref[0])
    gb = _cumsum_rows(tri_upper, hib_ref[0], lob_ref[0])

    attn = [None] * N_HEADS
    for block in LEVELS:
        q_l, k_l = _level_operands(q, kf, kb, gf, gb, block)
        for h, hs in enumerate(heads):
            p = _dot_nt(q_l[:, hs], k_l[:, hs])
            attn[h] = p if attn[h] is None else jnp.where(xor < 2 * block, p, attn[h])
    odd = (lax.broadcasted_iota(jnp.int32, q.shape, 0) & 1) != 0
    step = jnp.where(odd, hif_ref[0].astype(F32) + lof_ref[0].astype(F32),
                     hib_ref[0].astype(F32) + lob_ref[0].astype(F32))
    q_l = (q * jnp.exp2(step)).astype(BF16)
    k_l = jnp.where(odd, kb, kf).astype(BF16)
    k_d = (kf + kb).astype(BF16)
    for h, hs in enumerate(heads):
        attn[h] = jnp.where(xor < 2, _dot_nt(q_l[:, hs], k_l[:, hs]), attn[h])
        attn[h] = jnp.where(xor < 1, _dot_nt(q_ref[0, :, hs], k_d[:, hs]), attn[h])

    g_last = gf[c - 1:c, :]
    qdec = (q * jnp.exp2(gf)).astype(BF16)
    kdec = (kf * jnp.exp2(g_last - gf)).astype(BF16)
    decay = jnp.exp2(g_last)
    for h, hs in enumerate(heads):
        s_f = sf_ref[h]
        v = v_ref[0, :, hs]
        o = jnp.dot(attn[h].astype(BF16), v, preferred_element_type=F32)
        o_ref[0, :, hs] = (o + _dot_nt(qdec[:, hs], s_f.astype(BF16))).astype(o_ref.dtype)
        sf_ref[h] = s_f * decay[:, hs] + _dot_tn(v, kdec[:, hs])


def _gla(gates, k, hi, lo, val, batch, seq):
    n = seq // REC_CHUNK
    as3 = lambda a: a.reshape(batch, seq, a.shape[-1])

    def fwd(colblock):
        return pl.BlockSpec((1, REC_CHUNK, D_MODEL), lambda bi, ni: (bi, ni, colblock))

    def bwd(colblock):
        return pl.BlockSpec((1, REC_CHUNK, D_MODEL), lambda bi, ni: (bi, n - 1 - ni, colblock))

    out_shape = jax.ShapeDtypeStruct((batch, seq, D_MODEL), F32)
    g3, k3, hi3, lo3, v3 = as3(gates), as3(k), as3(hi), as3(lo), as3(val)
    return pl.pallas_call(
        _gla_kernel,
        grid=(batch, n),
        in_specs=[fwd(0), fwd(0), fwd(1), fwd(0), fwd(1), fwd(0), fwd(1), fwd(0),
                  bwd(0), bwd(1), bwd(1), bwd(1), bwd(0)],
        out_specs=[pl.BlockSpec((1, REC_CHUNK, D_MODEL), lambda bi, ni: (bi, ni, 0)),
                   pl.BlockSpec((1, REC_CHUNK, D_MODEL), lambda bi, ni: (bi, n - 1 - ni, 0))],
        out_shape=[out_shape, out_shape],
        scratch_shapes=[pltpu.VMEM((N_HEADS, HEAD_DIM, HEAD_DIM), F32),
                        pltpu.VMEM((N_HEADS, HEAD_DIM, HEAD_DIM), F32)],
        compiler_params=_compiler_params(("parallel", "arbitrary")),
        name="gla",
    )(g3, k3, k3, hi3, hi3, lo3, lo3, v3,
      g3, k3, hi3, lo3, v3)


def _merge_kernel(o1_ref, o2_ref, ga_ref, u_ref, v_ref, gb_ref, ma_ref, mb_ref, x_ref,
                  gnw_ref, ws_ref, bs_ref, wa_ref, wb_ref, wo_ref, nw_ref,
                  xo_ref, ho_ref):
    rows = x_ref.shape[0]

    o = o1_ref[...] + o2_ref[...]
    parts = []
    for h in range(N_HEADS):
        hs = slice(h * HEAD_DIM, (h + 1) * HEAD_DIM)
        oh = o[:, hs]
        oh = oh * lax.rsqrt(jnp.mean(oh * oh, axis=-1, keepdims=True) + RMS_EPS) * gnw_ref[...]
        parts.append((oh * ga_ref[:, hs].astype(F32)).astype(BF16))
    y_a = jnp.concatenate(parts, axis=1)

    row_parts = []
    for ci in range(rows // SG_CHUNK):
        rs = slice(ci * SG_CHUNK, (ci + 1) * SG_CHUNK)
        parts = []
        for g in range(SG_GROUPS):
            hs = slice(g * HEAD_DIM, (g + 1) * HEAD_DIM)
            mixed = jnp.dot(ws_ref[g], v_ref[rs, hs], preferred_element_type=F32) + bs_ref[g]
            parts.append((u_ref[rs, hs].astype(F32) * mixed * gb_ref[rs, hs].astype(F32)).astype(BF16))
        row_parts.append(jnp.concatenate(parts, axis=1))
    y_b = jnp.concatenate(row_parts, axis=0)

    merged = (ma_ref[...].astype(F32) * jnp.dot(y_a, wa_ref[...], preferred_element_type=F32)
              + mb_ref[...].astype(F32) * jnp.dot(y_b, wb_ref[...], preferred_element_type=F32))
    x_new = x_ref[...] + jnp.dot(merged.astype(BF16), wo_ref[...], preferred_element_type=F32)
    xo_ref[...] = x_new
    y = x_new * lax.rsqrt(jnp.mean(x_new * x_new, axis=-1, keepdims=True) + RMS_EPS)
    ho_ref[...] = (y * nw_ref[...]).astype(ho_ref.dtype)


def _merge(o1, o2, gates, u, v, merges, x, gnorm_w, w_s, b_s, w_a, w_b, w_o, next_norm_w, h_dtype, block_rows=256):
    t = x.shape[0]

    def tok(colblock=0):
        return pl.BlockSpec((block_rows, D_MODEL), lambda i: (i, colblock))

    def whole(shape):
        return pl.BlockSpec(shape, lambda i: (0,) * len(shape))

    bs_full = jnp.broadcast_to(b_s.astype(F32)[:, :, None], (SG_GROUPS, SG_CHUNK, HEAD_DIM))
    return pl.pallas_call(
        _merge_kernel,
        grid=(t // block_rows,),
        in_specs=[tok(), tok(), tok(1), tok(), tok(), tok(2), tok(0), tok(1), tok(),
                  whole((1, HEAD_DIM)),
                  whole((SG_GROUPS, SG_CHUNK, SG_CHUNK)), whole((SG_GROUPS, SG_CHUNK, HEAD_DIM)),
                  whole((D_MODEL, D_MODEL)), whole((D_MODEL, D_MODEL)), whole((D_MODEL, D_MODEL)),
                  whole((1, D_MODEL))],
        out_specs=[tok(), tok()],
        out_shape=[jax.ShapeDtypeStruct((t, D_MODEL), F32), jax.ShapeDtypeStruct((t, D_MODEL), h_dtype)],
        compiler_params=_compiler_params(("parallel",)),
        name="merge",
    )(o1, o2, gates, u, v, gates, merges, merges, x,
      gnorm_w.reshape(1, HEAD_DIM), w_s.astype(BF16), bs_full, w_a, w_b, w_o, next_norm_w.reshape(1, D_MODEL))


def kernel(x, norm_w, w_in, lower_bounds, gnorm_w, ln_w, ln_b, w_s, b_s, w_proj_a, w_proj_b, w_out, final_norm_w):
    batch, seq, _ = x.shape
    t = batch * seq
    lb = _lower_bounds(lower_bounds)
    x2 = x.reshape(t, D_MODEL)
    h = _rmsnorm(x2, norm_w[0], BF16)
    for layer in range(DEPTH):
        last = layer == DEPTH - 1
        gates, k, hi, lo, val, u, v, merges = _in_proj(h, w_in[layer].astype(BF16), lb[layer], ln_w[layer], ln_b[layer])
        o1, o2 = _gla(gates, k, hi, lo, val, batch, seq)
        x2, h = _merge(o1.reshape(t, D_MODEL), o2.reshape(t, D_MODEL), gates, u, v, merges, x2,
                       gnorm_w[layer], w_s[layer], b_s[layer],
                       w_proj_a[layer].astype(BF16), w_proj_b[layer].astype(BF16), w_out[layer].astype(BF16),
                       final_norm_w if last else norm_w[layer + 1], F32 if last else BF16)
    return h.reshape(batch, seq, D_MODEL)
```

```python
import math

import jax
import jax.numpy as jnp
from jax import lax
from jax.experimental import pallas as pl
from jax.experimental.pallas import tpu as pltpu

D_MODEL = 1024
DEPTH = 4
HEAD_DIM = 128
N_HEADS = D_MODEL // HEAD_DIM
SG_CHUNK = 128
SG_GROUPS = D_MODEL // HEAD_DIM
RMS_EPS = 1e-6
LN_EPS = 1e-5
LB_FLOOR = 1e-20

COL_Q, COL_FF, COL_FB, COL_I, COL_GA, COL_U, COL_V, COL_GB, COL_MA, COL_MB = range(10)

REC_CHUNK = 128
LEVELS = (64, 32, 16, 8, 4, 2)
GLA_STEP_CHUNKS = 4

PROJ_BLOCK_ROWS = 2048
LOG2E = math.log2(math.e)
VMEM_LIMIT_BYTES = 56 * 1024 * 1024

F32 = jnp.float32
BF16 = jnp.bfloat16


def _compiler_params(semantics):
    return pltpu.CompilerParams(dimension_semantics=semantics, vmem_limit_bytes=VMEM_LIMIT_BYTES)


def _lower_bound_kernel(lb_ref, out_ref):
    x = lb_ref[...]
    m = jnp.max(x, axis=0, keepdims=True)
    e = jnp.exp(x - m)
    p = e / jnp.sum(e, axis=0, keepdims=True)
    acc = p[0:1]
    rows = [acc - p[0:1]]
    for layer in range(1, DEPTH):
        acc = acc + p[layer:layer + 1]
        rows.append(acc - p[0:1])
    out_ref[...] = jnp.concatenate(rows, axis=0)


def _lower_bounds(lower_bounds):
    flat = lower_bounds.astype(F32).reshape(DEPTH, 2 * D_MODEL)
    out = pl.pallas_call(
        _lower_bound_kernel,
        out_shape=jax.ShapeDtypeStruct((DEPTH, 2 * D_MODEL), F32),
        name="lower_bounds",
    )(flat)
    return out.reshape(DEPTH, 2, 1, D_MODEL)


def _rmsnorm_kernel(x_ref, w_ref, o_ref):
    x = x_ref[...]
    y = x * lax.rsqrt(jnp.mean(x * x, axis=-1, keepdims=True) + RMS_EPS)
    o_ref[...] = (y * w_ref[...]).astype(o_ref.dtype)


def _rmsnorm(x2d, w, out_dtype, block_rows=1024):
    t = x2d.shape[0]
    return pl.pallas_call(
        _rmsnorm_kernel,
        grid=(t // block_rows,),
        in_specs=[pl.BlockSpec((block_rows, D_MODEL), lambda i: (i, 0)),
                  pl.BlockSpec((1, D_MODEL), lambda i: (0, 0))],
        out_specs=pl.BlockSpec((block_rows, D_MODEL), lambda i: (i, 0)),
        out_shape=jax.ShapeDtypeStruct((t, D_MODEL), out_dtype),
        compiler_params=_compiler_params(("parallel",)),
        name="rmsnorm",
    )(x2d, w.reshape(1, D_MODEL))


def _sigmoid(x):
    return 1.0 / (1.0 + jnp.exp2(x * (-LOG2E)))


def _gelu(x):
    a = -2.0 * LOG2E * math.sqrt(2.0 / math.pi)
    t = (x * x) * (a * 0.044715) + a
    return x / (1.0 + jnp.exp2(x * t))


def _project_rows(h_ref, w_ref, emit, sub_rows):
    blocks = [slice(r, r + sub_rows) for r in range(0, h_ref.shape[0], sub_rows)]
    acc = jnp.dot(h_ref[blocks[0], :], w_ref[...], preferred_element_type=F32)
    for prev, rows in zip(blocks[:-1], blocks[1:]):
        nxt = jnp.dot(h_ref[rows, :], w_ref[...], preferred_element_type=F32)
        emit(prev, acc)
        acc = nxt
    emit(blocks[-1], acc)


def _proj_plain_kernel(h_ref, w_ref, o_ref):
    def emit(rows, acc):
        o_ref[rows, :] = acc.astype(o_ref.dtype)
    _project_rows(h_ref, w_ref, emit, PROJ_BLOCK_ROWS)


def _proj_silu_kernel(h_ref, w_ref, scale_ref, o_ref):
    def emit(rows, acc):
        o_ref[rows, :] = (acc * _sigmoid(acc) * scale_ref[0]).astype(o_ref.dtype)
    _project_rows(h_ref, w_ref, emit, 256)


def _proj_sigmoid_kernel(h_ref, w_ref, o_ref):
    def emit(rows, acc):
        o_ref[rows, :] = _sigmoid(acc).astype(o_ref.dtype)
    _project_rows(h_ref, w_ref, emit, 128)


def _proj_gelu_kernel(h_ref, w_ref, o_ref):
    def emit(rows, acc):
        o_ref[rows, :] = _gelu(acc).astype(o_ref.dtype)
    _project_rows(h_ref, w_ref, emit, 128)


def _proj_gelu_ln_kernel(h_ref, w_ref, lnw_ref, lnb_ref, o_ref):
    def emit(rows, acc):
        v = _gelu(acc)
        vc = v - jnp.mean(v, axis=-1, keepdims=True)
        var = jnp.mean(vc * vc, axis=-1, keepdims=True)
        o_ref[rows, :] = (vc * lax.rsqrt(var + LN_EPS) * lnw_ref[...] + lnb_ref[...]).astype(o_ref.dtype)
    _project_rows(h_ref, w_ref, emit, 256)


def _proj_forget_kernel(h_ref, w_ref, lb_ref, k_ref, hi_ref, lo_ref):
    lb = lb_ref[0]
    floor_lb, one_minus_lb = jnp.maximum(lb, LB_FLOOR), 1.0 - lb

    def emit(rows, acc):
        gated = one_minus_lb * _sigmoid(acc)
        k_ref[rows, :] = (one_minus_lb - gated).astype(k_ref.dtype)
        l2f = jnp.log2(floor_lb + gated)
        hi = l2f.astype(BF16)
        hi_ref[rows, :] = hi
        lo_ref[rows, :] = (l2f - hi.astype(F32)).astype(BF16)
    _project_rows(h_ref, w_ref, emit, 128)


def _proj(kernel_fn, name, h, w, col_of, n_cols, n_out=1, extra=()):
    t = h.shape[0]
    out_spec = pl.BlockSpec((PROJ_BLOCK_ROWS, D_MODEL), lambda j, i: (i, j))
    out_shape = jax.ShapeDtypeStruct((t, n_cols * D_MODEL), BF16)
    res = pl.pallas_call(
        kernel_fn,
        grid=(n_cols, t // PROJ_BLOCK_ROWS),
        in_specs=[pl.BlockSpec((PROJ_BLOCK_ROWS, D_MODEL), lambda j, i: (i, 0)),
                  pl.BlockSpec((D_MODEL, D_MODEL), lambda j, i: (0, col_of(j)))]
                 + [pl.BlockSpec(shape, imap) for _, shape, imap in extra],
        out_specs=[out_spec] * n_out,
        out_shape=[out_shape] * n_out,
        compiler_params=_compiler_params(("parallel", "parallel")),
        name=name,
    )(h, w, *[a for a, _, _ in extra])
    return res if n_out > 1 else res[0]


def _in_proj(h, w, lb_layer, ln_w, ln_b):
    row = lambda a: a.reshape(1, D_MODEL)
    const = lambda j, i: (0, 0)
    scales = jnp.ones((3, 1, D_MODEL), F32).at[0].multiply(HEAD_DIM ** -0.5)
    gates = _proj(_proj_silu_kernel, "proj_silu", h, w, lambda j: (j * 7 + 1) // 2, 3,
                  extra=[(scales, (1, 1, D_MODEL), lambda j, i: (j, 0, 0))])
    k, hi, lo = _proj(_proj_forget_kernel, "proj_forget", h, w, lambda j: j + COL_FF, 2, n_out=3,
                      extra=[(lb_layer, (1, 1, D_MODEL), lambda j, i: (j, 0, 0))])
    val = _proj(_proj_plain_kernel, "proj_plain", h, w, lambda j: j + COL_I, 1)
    u = _proj(_proj_gelu_kernel, "proj_gelu", h, w, lambda j: j + COL_U, 1)
    v = _proj(_proj_gelu_ln_kernel, "proj_gelu_ln", h, w, lambda j: j + COL_V, 1,
              extra=[(row(ln_w), (1, D_MODEL), const), (row(ln_b), (1, D_MODEL), const)])
    merges = _proj(_proj_sigmoid_kernel, "proj_sigmoid", h, w, lambda j: j + COL_MA, 2)
    return gates, k, hi, lo, val, u, v, merges


def _dot_nt(a, b):
    return lax.dot_general(a, b, (((1,), (1,)), ((), ())), preferred_element_type=F32)


def _dot_tn(a, b):
    return lax.dot_general(a, b, (((0,), (0,)), ((), ())), preferred_element_type=F32)


def _cumsum_rows(tri, hi, lo):
    width = hi.shape[1]
    r = jnp.dot(tri, jnp.concatenate([hi, lo], axis=1), preferred_element_type=F32)
    return r[:, :width] + r[:, width:]


def _boundary_rows(g, block, row_in_parent):
    c, width = g.shape
    parent = 2 * block
    g3 = g.reshape(c // 8, 8, width)
    sub = lax.broadcasted_iota(jnp.int32, g3.shape, 1)
    r = jnp.broadcast_to(g3[:, row_in_parent:row_in_parent + 1, :], g3.shape)
    for p in range(1, 8 // parent):
        row = p * parent + row_in_parent
        r = jnp.where(sub >= p * parent, jnp.broadcast_to(g3[:, row:row + 1, :], g3.shape), r)
    return r.reshape(c, width)


def _level_operands(q, kf, kb, gf, gb, block):
    c = q.shape[0]
    if block >= 8:
        q_parts, k_parts = [], []
        for blk in range(c // block):
            rows = slice(blk * block, (blk + 1) * block)
            edge = (blk | 1) * block
            if blk % 2:
                e_q = gf[rows] - gf[edge - 1:edge]
                e_k = gb[edge:edge + 1] - gb[rows]
                k_src = kb
            else:
                e_q = gb[rows] - gb[edge:edge + 1]
                e_k = gf[edge - 1:edge] - gf[rows]
                k_src = kf
            q_parts.append(q[rows] * jnp.exp2(e_q))
            k_parts.append(k_src[rows] * jnp.exp2(e_k))
        q_l = jnp.concatenate(q_parts, axis=0)
        k_l = jnp.concatenate(k_parts, axis=0)
    else:
        d_f = gf - _boundary_rows(gf, block, block - 1)
        d_b = gb - _boundary_rows(gb, block, block)
        odd = (lax.broadcasted_iota(jnp.int32, q.shape, 0) & block) != 0
        q_l = q * jnp.exp2(jnp.minimum(d_f, d_b))
        k_l = jnp.where(odd, kb, kf) * jnp.exp2(-jnp.maximum(d_f, d_b))
    return q_l.astype(BF16), k_l.astype(BF16)


def _gla_kernel(q_ref, kf_ref, kb_ref, hif_ref, hib_ref, lof_ref, lob_ref, v_ref,
                q2_ref, kb2_ref, hib2_ref, lob2_ref, v2_ref,
                o_ref, o2_ref, sf_ref, sb_ref):
    c = REC_CHUNK

    @pl.when(pl.program_id(1) == 0)
    def _():
        sf_ref[...] = jnp.zeros_like(sf_ref)
        sb_ref[...] = jnp.zeros_like(sb_ref)

    row = lax.broadcasted_iota(jnp.int32, (c, c), 0)
    col = lax.broadcasted_iota(jnp.int32, (c, c), 1)
    tri_lower = jnp.where(col <= row, 1.0, 0.0).astype(BF16)
    tri_upper = jnp.where(col >= row, 1.0, 0.0).astype(BF16)
    xor = row ^ col
    heads = [slice(h * HEAD_DIM, (h + 1) * HEAD_DIM) for h in range(N_HEADS)]

    def chunk(ci, carry):
        rows = pl.ds(pl.multiple_of(ci * c, c), c)
        rows2 = pl.ds(pl.multiple_of((GLA_STEP_CHUNKS - 1 - ci) * c, c), c)

        gb2 = _cumsum_rows(tri_upper, hib2_ref[0, rows2, :], lob2_ref[0, rows2, :])
        g_first = gb2[0:1, :]
        qdec2 = (q2_ref[0, rows2, :].astype(F32) * jnp.exp2(gb2)).astype(BF16)
        kdec2 = (kb2_ref[0, rows2, :].astype(F32) * jnp.exp2(g_first - gb2)).astype(BF16)
        decay2 = jnp.exp2(g_first)
        for h, hs in enumerate(heads):
            s_b = sb_ref[h]
            o2_ref[0, rows2, hs] = _dot_nt(qdec2[:, hs], s_b.astype(BF16)).astype(o2_ref.dtype)
            sb_ref[h] = s_b * decay2[:, hs] + _dot_tn(v2_ref[0, rows2, hs], kdec2[:, hs])

        q16 = q_ref[0, rows, :]
        q = q16.astype(F32)
        kf = kf_ref[0, rows, :].astype(F32)
        kb = kb_ref[0, rows, :].astype(F32)
        hif, lof = hif_ref[0, rows, :], lof_ref[0, rows, :]
        hib, lob = hib_ref[0, rows, :], lob_ref[0, rows, :]
        gf = _cumsum_rows(tri_lower, hif, lof)
        gb = _cumsum_rows(tri_upper, hib, lob)

        attn = [None] * N_HEADS
        for block in LEVELS:
            q_l, k_l = _level_operands(q, kf, kb, gf, gb, block)
            for h, hs in enumerate(heads):
                p = _dot_nt(q_l[:, hs], k_l[:, hs])
                attn[h] = p if attn[h] is None else jnp.where(xor < 2 * block, p, attn[h])
        odd = (lax.broadcasted_iota(jnp.int32, q.shape, 0) & 1) != 0
        step = jnp.where(odd, hif.astype(F32) + lof.astype(F32), hib.astype(F32) + lob.astype(F32))
        q_l = (q * jnp.exp2(step)).astype(BF16)
        k_l = jnp.where(odd, kb, kf).astype(BF16)
        k_d = (kf + kb).astype(BF16)
        for h, hs in enumerate(heads):
            attn[h] = jnp.where(xor < 2, _dot_nt(q_l[:, hs], k_l[:, hs]), attn[h])
            attn[h] = jnp.where(xor < 1, _dot_nt(q16[:, hs], k_d[:, hs]), attn[h])

        g_last = gf[c - 1:c, :]
        qdec = (q * jnp.exp2(gf)).astype(BF16)
        kdec = (kf * jnp.exp2(g_last - gf)).astype(BF16)
        decay = jnp.exp2(g_last)
        for h, hs in enumerate(heads):
            s_f = sf_ref[h]
            v = v_ref[0, rows, hs]
            o = jnp.dot(attn[h].astype(BF16), v, preferred_element_type=F32)
            o_ref[0, rows, hs] = (o + _dot_nt(qdec[:, hs], s_f.astype(BF16))).astype(o_ref.dtype)
            sf_ref[h] = s_f * decay[:, hs] + _dot_tn(v, kdec[:, hs])
        return carry

    lax.fori_loop(0, GLA_STEP_CHUNKS, chunk, 0)


def _gla(gates, k, hi, lo, val, batch, seq):
    rows = GLA_STEP_CHUNKS * REC_CHUNK
    n = seq // rows
    as3 = lambda a: a.reshape(batch, seq, a.shape[-1])

    def fwd(colblock):
        return pl.BlockSpec((1, rows, D_MODEL), lambda bi, ni: (bi, ni, colblock))

    def bwd(colblock):
        return pl.BlockSpec((1, rows, D_MODEL), lambda bi, ni: (bi, n - 1 - ni, colblock))

    out_shape = jax.ShapeDtypeStruct((batch, seq, D_MODEL), BF16)
    g3, k3, hi3, lo3, v3 = as3(gates), as3(k), as3(hi), as3(lo), as3(val)
    return pl.pallas_call(
        _gla_kernel,
        grid=(batch, n),
        in_specs=[fwd(0), fwd(0), fwd(1), fwd(0), fwd(1), fwd(0), fwd(1), fwd(0),
                  bwd(0), bwd(1), bwd(1), bwd(1), bwd(0)],
        out_specs=[fwd(0), bwd(0)],
        out_shape=[out_shape, out_shape],
        scratch_shapes=[pltpu.VMEM((N_HEADS, HEAD_DIM, HEAD_DIM), F32),
                        pltpu.VMEM((N_HEADS, HEAD_DIM, HEAD_DIM), F32)],
        compiler_params=_compiler_params(("parallel", "arbitrary")),
        name="gla",
    )(g3, k3, k3, hi3, hi3, lo3, lo3, v3,
      g3, k3, hi3, lo3, v3)


def _merge_kernel(o1_ref, o2_ref, ga_ref, u_ref, v_ref, gb_ref, ma_ref, mb_ref, x_ref,
                  gnw_ref, ws_ref, bs_ref, wa_ref, wb_ref, wo_ref, nw_ref,
                  xo_ref, ho_ref):
    rows = x_ref.shape[0]

    o = o1_ref[...].astype(F32) + o2_ref[...].astype(F32)
    parts = []
    for h in range(N_HEADS):
        hs = slice(h * HEAD_DIM, (h + 1) * HEAD_DIM)
        oh = o[:, hs]
        oh = oh * lax.rsqrt(jnp.mean(oh * oh, axis=-1, keepdims=True) + RMS_EPS) * gnw_ref[...]
        parts.append((oh * ga_ref[:, hs].astype(F32)).astype(BF16))
    y_a = jnp.concatenate(parts, axis=1)

    row_parts = []
    for ci in range(rows // SG_CHUNK):
        rs = slice(ci * SG_CHUNK, (ci + 1) * SG_CHUNK)
        parts = []
        for g in range(SG_GROUPS):
            hs = slice(g * HEAD_DIM, (g + 1) * HEAD_DIM)
            mixed = jnp.dot(ws_ref[g], v_ref[rs, hs], preferred_element_type=F32) + bs_ref[g]
            parts.append((u_ref[rs, hs].astype(F32) * mixed * gb_ref[rs, hs].astype(F32)).astype(BF16))
        row_parts.append(jnp.concatenate(parts, axis=1))
    y_b = jnp.concatenate(row_parts, axis=0)

    merged = (ma_ref[...].astype(F32) * jnp.dot(y_a, wa_ref[...], preferred_element_type=F32)
              + mb_ref[...].astype(F32) * jnp.dot(y_b, wb_ref[...], preferred_element_type=F32))
    x_new = x_ref[...] + jnp.dot(merged.astype(BF16), wo_ref[...], preferred_element_type=F32)
    xo_ref[...] = x_new
    y = x_new * lax.rsqrt(jnp.mean(x_new * x_new, axis=-1, keepdims=True) + RMS_EPS)
    ho_ref[...] = (y * nw_ref[...]).astype(ho_ref.dtype)


def _merge(o1, o2, gates, u, v, merges, x, gnorm_w, w_s, b_s, w_a, w_b, w_o, next_norm_w, h_dtype, block_rows=512):
    t = x.shape[0]

    def tok(colblock=0):
        return pl.BlockSpec((block_rows, D_MODEL), lambda i: (i, colblock))

    def whole(shape):
        return pl.BlockSpec(shape, lambda i: (0,) * len(shape))

    bs_full = jnp.broadcast_to(b_s.astype(F32)[:, :, None], (SG_GROUPS, SG_CHUNK, HEAD_DIM))
    return pl.pallas_call(
        _merge_kernel,
        grid=(t // block_rows,),
        in_specs=[tok(), tok(), tok(1), tok(), tok(), tok(2), tok(0), tok(1), tok(),
                  whole((1, HEAD_DIM)),
                  whole((SG_GROUPS, SG_CHUNK, SG_CHUNK)), whole((SG_GROUPS, SG_CHUNK, HEAD_DIM)),
                  whole((D_MODEL, D_MODEL)), whole((D_MODEL, D_MODEL)), whole((D_MODEL, D_MODEL)),
                  whole((1, D_MODEL))],
        out_specs=[tok(), tok()],
        out_shape=[jax.ShapeDtypeStruct((t, D_MODEL), F32), jax.ShapeDtypeStruct((t, D_MODEL), h_dtype)],
        compiler_params=_compiler_params(("parallel",)),
        name="merge",
    )(o1, o2, gates, u, v, gates, merges, merges, x,
      gnorm_w.reshape(1, HEAD_DIM), w_s.astype(BF16), bs_full, w_a, w_b, w_o, next_norm_w.reshape(1, D_MODEL))


def kernel(x, norm_w, w_in, lower_bounds, gnorm_w, ln_w, ln_b, w_s, b_s, w_proj_a, w_proj_b, w_out, final_norm_w):
    batch, seq, _ = x.shape
    t = batch * seq
    lb = _lower_bounds(lower_bounds)
    x2 = x.reshape(t, D_MODEL)
    h = _rmsnorm(x2, norm_w[0], BF16)
    for layer in range(DEPTH):
        last = layer == DEPTH - 1
        gates, k, hi, lo, val, u, v, merges = _in_proj(h, w_in[layer].astype(BF16), lb[layer], ln_w[layer], ln_b[layer])
        o1, o2 = _gla(gates, k, hi, lo, val, batch, seq)
        x2, h = _merge(o1.reshape(t, D_MODEL), o2.reshape(t, D_MODEL), gates, u, v, merges, x2,
                       gnorm_w[layer], w_s[layer], b_s[layer],
                       w_proj_a[layer].astype(BF16), w_proj_b[layer].astype(BF16), w_out[layer].astype(BF16),
                       final_norm_w if last else norm_w[layer + 1], F32 if last else BF16)
    return h.reshape(batch, seq, D_MODEL)
```

```python
import math

import jax
import jax.numpy as jnp
from jax import lax
from jax.experimental import pallas as pl
from jax.experimental.pallas import tpu as pltpu

D_MODEL = 1024
DEPTH = 4
HEAD_DIM = 128
N_HEADS = D_MODEL // HEAD_DIM
SG_CHUNK = 128
SG_GROUPS = D_MODEL // HEAD_DIM
RMS_EPS = 1e-6
LN_EPS = 1e-5
LB_FLOOR = 1e-20

COL_Q, COL_FF, COL_FB, COL_I, COL_GA, COL_U, COL_V, COL_GB, COL_MA, COL_MB = range(10)

REC_CHUNK = 128
LEVELS = (64, 32, 16, 8, 4, 2)
GLA_STEP_CHUNKS = 4
BF16_ROWS = 16

PROJ_BLOCK_ROWS = 2048
LOG2E = math.log2(math.e)
VMEM_LIMIT_BYTES = 56 * 1024 * 1024

F32 = jnp.float32
BF16 = jnp.bfloat16


def _compiler_params(semantics):
    return pltpu.CompilerParams(dimension_semantics=semantics, vmem_limit_bytes=VMEM_LIMIT_BYTES)


def _lower_bound_kernel(lb_ref, out_ref):
    x = lb_ref[...]
    m = jnp.max(x, axis=0, keepdims=True)
    e = jnp.exp(x - m)
    p = e / jnp.sum(e, axis=0, keepdims=True)
    acc = p[0:1]
    rows = [acc - p[0:1]]
    for layer in range(1, DEPTH):
        acc = acc + p[layer:layer + 1]
        rows.append(acc - p[0:1])
    out_ref[...] = jnp.concatenate(rows, axis=0)


def _lower_bounds(lower_bounds):
    flat = lower_bounds.astype(F32).reshape(DEPTH, 2 * D_MODEL)
    out = pl.pallas_call(
        _lower_bound_kernel,
        out_shape=jax.ShapeDtypeStruct((DEPTH, 2 * D_MODEL), F32),
        name="lower_bounds",
    )(flat)
    return out.reshape(DEPTH, 2, 1, D_MODEL)


def _rmsnorm_kernel(x_ref, w_ref, o_ref):
    x = x_ref[...]
    y = x * lax.rsqrt(jnp.mean(x * x, axis=-1, keepdims=True) + RMS_EPS)
    o_ref[...] = (y * w_ref[...]).astype(o_ref.dtype)


def _rmsnorm(x2d, w, out_dtype, block_rows=1024):
    t = x2d.shape[0]
    return pl.pallas_call(
        _rmsnorm_kernel,
        grid=(t // block_rows,),
        in_specs=[pl.BlockSpec((block_rows, D_MODEL), lambda i: (i, 0)),
                  pl.BlockSpec((1, D_MODEL), lambda i: (0, 0))],
        out_specs=pl.BlockSpec((block_rows, D_MODEL), lambda i: (i, 0)),
        out_shape=jax.ShapeDtypeStruct((t, D_MODEL), out_dtype),
        compiler_params=_compiler_params(("parallel",)),
        name="rmsnorm",
    )(x2d, w.reshape(1, D_MODEL))


def _sigmoid(x):
    return 1.0 / (1.0 + jnp.exp2(x * (-LOG2E)))


def _gelu(x):
    a = -2.0 * LOG2E * math.sqrt(2.0 / math.pi)
    t = (x * x) * (a * 0.044715) + a
    return x / (1.0 + jnp.exp2(x * t))


def _project_rows(h_ref, w_ref, emit, sub_rows):
    blocks = [slice(r, r + sub_rows) for r in range(0, h_ref.shape[0], sub_rows)]
    acc = jnp.dot(h_ref[blocks[0], :], w_ref[...], preferred_element_type=F32)
    for prev, rows in zip(blocks[:-1], blocks[1:]):
        nxt = jnp.dot(h_ref[rows, :], w_ref[...], preferred_element_type=F32)
        emit(prev, acc)
        acc = nxt
    emit(blocks[-1], acc)


def _proj_plain_kernel(h_ref, w_ref, o_ref):
    def emit(rows, acc):
        o_ref[rows, :] = acc.astype(o_ref.dtype)
    _project_rows(h_ref, w_ref, emit, PROJ_BLOCK_ROWS)


def _proj_silu_kernel(h_ref, w_ref, scale_ref, o_ref):
    def emit(rows, acc):
        o_ref[rows, :] = (acc * _sigmoid(acc) * scale_ref[0]).astype(o_ref.dtype)
    _project_rows(h_ref, w_ref, emit, 256)


def _proj_sigmoid_kernel(h_ref, w_ref, o_ref):
    def emit(rows, acc):
        o_ref[rows, :] = _sigmoid(acc).astype(o_ref.dtype)
    _project_rows(h_ref, w_ref, emit, 128)


def _proj_gelu_kernel(h_ref, w_ref, o_ref):
    def emit(rows, acc):
        o_ref[rows, :] = _gelu(acc).astype(o_ref.dtype)
    _project_rows(h_ref, w_ref, emit, 128)


def _proj_gelu_ln_kernel(h_ref, w_ref, lnw_ref, lnb_ref, o_ref):
    def emit(rows, acc):
        v = _gelu(acc)
        vc = v - jnp.mean(v, axis=-1, keepdims=True)
        var = jnp.mean(vc * vc, axis=-1, keepdims=True)
        o_ref[rows, :] = (vc * lax.rsqrt(var + LN_EPS) * lnw_ref[...] + lnb_ref[...]).astype(o_ref.dtype)
    _project_rows(h_ref, w_ref, emit, 256)


def _proj_forget_kernel(h_ref, w_ref, lb_ref, k_ref, l2f_ref):
    lb = lb_ref[0]
    floor_lb, one_minus_lb = jnp.maximum(lb, LB_FLOOR), 1.0 - lb

    def emit(rows, acc):
        gated = one_minus_lb * _sigmoid(acc)
        k_ref[rows, :] = (one_minus_lb - gated).astype(k_ref.dtype)
        l2f_ref[rows, :] = jnp.log2(floor_lb + gated).astype(l2f_ref.dtype)
    _project_rows(h_ref, w_ref, emit, 128)


def _proj(kernel_fn, name, h, w, col_of, n_cols, n_out=1, extra=()):
    t = h.shape[0]
    out_spec = pl.BlockSpec((PROJ_BLOCK_ROWS, D_MODEL), lambda j, i: (i, j))
    out_shape = jax.ShapeDtypeStruct((t, n_cols * D_MODEL), BF16)
    res = pl.pallas_call(
        kernel_fn,
        grid=(n_cols, t // PROJ_BLOCK_ROWS),
        in_specs=[pl.BlockSpec((PROJ_BLOCK_ROWS, D_MODEL), lambda j, i: (i, 0)),
                  pl.BlockSpec((D_MODEL, D_MODEL), lambda j, i: (0, col_of(j)))]
                 + [pl.BlockSpec(shape, imap) for _, shape, imap in extra],
        out_specs=[out_spec] * n_out,
        out_shape=[out_shape] * n_out,
        compiler_params=_compiler_params(("parallel", "parallel")),
        name=name,
    )(h, w, *[a for a, _, _ in extra])
    return res if n_out > 1 else res[0]


def _in_proj(h, w, lb_layer, ln_w, ln_b):
    row = lambda a: a.reshape(1, D_MODEL)
    const = lambda j, i: (0, 0)
    scales = jnp.ones((3, 1, D_MODEL), F32).at[0].multiply(HEAD_DIM ** -0.5)
    gates = _proj(_proj_silu_kernel, "proj_silu", h, w, lambda j: (j * 7 + 1) // 2, 3,
                  extra=[(scales, (1, 1, D_MODEL), lambda j, i: (j, 0, 0))])
    k, l2f = _proj(_proj_forget_kernel, "proj_forget", h, w, lambda j: j + COL_FF, 2, n_out=2,
                   extra=[(lb_layer, (1, 1, D_MODEL), lambda j, i: (j, 0, 0))])
    val = _proj(_proj_plain_kernel, "proj_plain", h, w, lambda j: j + COL_I, 1)
    u = _proj(_proj_gelu_kernel, "proj_gelu", h, w, lambda j: j + COL_U, 1)
    v = _proj(_proj_gelu_ln_kernel, "proj_gelu_ln", h, w, lambda j: j + COL_V, 1,
              extra=[(row(ln_w), (1, D_MODEL), const), (row(ln_b), (1, D_MODEL), const)])
    merges = _proj(_proj_sigmoid_kernel, "proj_sigmoid", h, w, lambda j: j + COL_MA, 2)
    return gates, k, l2f, val, u, v, merges


def _dot_nt(a, b):
    return lax.dot_general(a, b, (((1,), (1,)), ((), ())), preferred_element_type=F32)


def _dot_tn(a, b):
    return lax.dot_general(a, b, (((0,), (0,)), ((), ())), preferred_element_type=F32)


def _boundary_rows(g, block, row_in_parent):
    c, width = g.shape
    parent = 2 * block
    g3 = g.reshape(c // 8, 8, width)
    sub = lax.broadcasted_iota(jnp.int32, g3.shape, 1)
    r = jnp.broadcast_to(g3[:, row_in_parent:row_in_parent + 1, :], g3.shape)
    for p in range(1, 8 // parent):
        row = p * parent + row_in_parent
        r = jnp.where(sub >= p * parent, jnp.broadcast_to(g3[:, row:row + 1, :], g3.shape), r)
    return r.reshape(c, width)


def _level_operands(q16, kf16, kb16, q, kf, kb, gf, gb, block):
    c = q.shape[0]
    if block >= 8:
        eq_parts, ek_parts, k16_parts, k32_parts = [], [], [], []
        for blk in range(c // block):
            rows = slice(blk * block, (blk + 1) * block)
            edge = (blk | 1) * block
            if blk % 2:
                eq_parts.append(gf[rows] - gf[edge - 1:edge])
                ek_parts.append(gb[edge:edge + 1] - gb[rows])
                k16_parts.append(kb16[rows])
                k32_parts.append(kb[rows])
            else:
                eq_parts.append(gb[rows] - gb[edge:edge + 1])
                ek_parts.append(gf[edge - 1:edge] - gf[rows])
                k16_parts.append(kf16[rows])
                k32_parts.append(kf[rows])
        e_q = jnp.concatenate(eq_parts, axis=0)
        e_k = jnp.concatenate(ek_parts, axis=0)
        if block >= BF16_ROWS:
            q_l = q16 * jnp.exp2(e_q.astype(BF16))
            k_l = jnp.concatenate(k16_parts, axis=0) * jnp.exp2(e_k.astype(BF16))
            return q_l, k_l
        q_l = q * jnp.exp2(e_q)
        k_l = jnp.concatenate(k32_parts, axis=0) * jnp.exp2(e_k)
    else:
        d_f = gf - _boundary_rows(gf, block, block - 1)
        d_b = gb - _boundary_rows(gb, block, block)
        odd = (lax.broadcasted_iota(jnp.int32, q.shape, 0) & block) != 0
        q_l = q * jnp.exp2(jnp.minimum(d_f, d_b))
        k_l = jnp.where(odd, kb, kf) * jnp.exp2(-jnp.maximum(d_f, d_b))
    return q_l.astype(BF16), k_l.astype(BF16)


def _gla_kernel(q_ref, kf_ref, kb_ref, lf_ref, lb_ref, v_ref,
                q2_ref, kb2_ref, lb2_ref, v2_ref,
                o_ref, o2_ref, sf_ref, sb_ref):
    c = REC_CHUNK

    @pl.when(pl.program_id(1) == 0)
    def _():
        sf_ref[...] = jnp.zeros_like(sf_ref)
        sb_ref[...] = jnp.zeros_like(sb_ref)

    row = lax.broadcasted_iota(jnp.int32, (c, c), 0)
    col = lax.broadcasted_iota(jnp.int32, (c, c), 1)
    tri_lower = jnp.where(col <= row, 1.0, 0.0).astype(BF16)
    tri_upper = jnp.where(col >= row, 1.0, 0.0).astype(BF16)
    xor = row ^ col
    heads = [slice(h * HEAD_DIM, (h + 1) * HEAD_DIM) for h in range(N_HEADS)]

    def chunk(ci, carry):
        rows = pl.ds(pl.multiple_of(ci * c, c), c)
        rows2 = pl.ds(pl.multiple_of((GLA_STEP_CHUNKS - 1 - ci) * c, c), c)

        gb2 = jnp.dot(tri_upper, lb2_ref[0, rows2, :], preferred_element_type=F32)
        g_first = gb2[0:1, :]
        qdec2 = q2_ref[0, rows2, :] * jnp.exp2(gb2.astype(BF16))
        kdec2 = kb2_ref[0, rows2, :] * jnp.exp2((g_first - gb2).astype(BF16))
        decay2 = jnp.exp2(g_first)
        for h, hs in enumerate(heads):
            s_b = sb_ref[h]
            o2_ref[0, rows2, hs] = _dot_nt(qdec2[:, hs], s_b.astype(BF16)).astype(o2_ref.dtype)
            sb_ref[h] = s_b * decay2[:, hs] + _dot_tn(v2_ref[0, rows2, hs], kdec2[:, hs])

        q16, kf16, kb16 = q_ref[0, rows, :], kf_ref[0, rows, :], kb_ref[0, rows, :]
        q, kf, kb = q16.astype(F32), kf16.astype(F32), kb16.astype(F32)
        lf16, lb16 = lf_ref[0, rows, :], lb_ref[0, rows, :]
        gf = jnp.dot(tri_lower, lf16, preferred_element_type=F32)
        gb = jnp.dot(tri_upper, lb16, preferred_element_type=F32)

        attn = [None] * N_HEADS
        for block in LEVELS:
            q_l, k_l = _level_operands(q16, kf16, kb16, q, kf, kb, gf, gb, block)
            for h, hs in enumerate(heads):
                p = _dot_nt(q_l[:, hs], k_l[:, hs])
                attn[h] = p if attn[h] is None else jnp.where(xor < 2 * block, p, attn[h])
        odd = (lax.broadcasted_iota(jnp.int32, q.shape, 0) & 1) != 0
        step = jnp.where(odd, lf16.astype(F32), lb16.astype(F32))
        q_l = (q * jnp.exp2(step)).astype(BF16)
        k_l = jnp.where(odd, kb, kf).astype(BF16)
        k_d = (kf + kb).astype(BF16)
        for h, hs in enumerate(heads):
            attn[h] = jnp.where(xor < 2, _dot_nt(q_l[:, hs], k_l[:, hs]), attn[h])
            attn[h] = jnp.where(xor < 1, _dot_nt(q16[:, hs], k_d[:, hs]), attn[h])

        g_last = gf[c - 1:c, :]
        qdec = q16 * jnp.exp2(gf.astype(BF16))
        kdec = kf16 * jnp.exp2((g_last - gf).astype(BF16))
        decay = jnp.exp2(g_last)
        for h, hs in enumerate(heads):
            s_f = sf_ref[h]
            v = v_ref[0, rows, hs]
            o = jnp.dot(attn[h].astype(BF16), v, preferred_element_type=F32)
            o_ref[0, rows, hs] = (o + _dot_nt(qdec[:, hs], s_f.astype(BF16))).astype(o_ref.dtype)
            sf_ref[h] = s_f * decay[:, hs] + _dot_tn(v, kdec[:, hs])
        return carry

    lax.fori_loop(0, GLA_STEP_CHUNKS, chunk, 0)


def _gla(gates, k, l2f, val, batch, seq):
    rows = GLA_STEP_CHUNKS * REC_CHUNK
    n = seq // rows
    as3 = lambda a: a.reshape(batch, seq, a.shape[-1])

    def fwd(colblock):
        return pl.BlockSpec((1, rows, D_MODEL), lambda bi, ni: (bi, ni, colblock))

    def bwd(colblock):
        return pl.BlockSpec((1, rows, D_MODEL), lambda bi, ni: (bi, n - 1 - ni, colblock))

    out_shape = jax.ShapeDtypeStruct((batch, seq, D_MODEL), BF16)
    g3, k3, l3, v3 = as3(gates), as3(k), as3(l2f), as3(val)
    return pl.pallas_call(
        _gla_kernel,
        grid=(batch, n),
        in_specs=[fwd(0), fwd(0), fwd(1), fwd(0), fwd(1), fwd(0),
                  bwd(0), bwd(1), bwd(1), bwd(0)],
        out_specs=[fwd(0), bwd(0)],
        out_shape=[out_shape, out_shape],
        scratch_shapes=[pltpu.VMEM((N_HEADS, HEAD_DIM, HEAD_DIM), F32),
                        pltpu.VMEM((N_HEADS, HEAD_DIM, HEAD_DIM), F32)],
        compiler_params=_compiler_params(("parallel", "arbitrary")),
        name="gla",
    )(g3, k3, k3, l3, l3, v3,
      g3, k3, l3, v3)


def _merge_kernel(o1_ref, o2_ref, ga_ref, u_ref, v_ref, gb_ref, ma_ref, mb_ref, x_ref,
                  gnw_ref, ws_ref, bs_ref, wa_ref, wb_ref, wo_ref, nw_ref,
                  xo_ref, ho_ref):
    rows = x_ref.shape[0]

    o = o1_ref[...].astype(F32) + o2_ref[...].astype(F32)
    parts = []
    for h in range(N_HEADS):
        hs = slice(h * HEAD_DIM, (h + 1) * HEAD_DIM)
        oh = o[:, hs]
        oh = oh * lax.rsqrt(jnp.mean(oh * oh, axis=-1, keepdims=True) + RMS_EPS) * gnw_ref[...]
        parts.append((oh * ga_ref[:, hs].astype(F32)).astype(BF16))
    y_a = jnp.concatenate(parts, axis=1)

    row_parts = []
    for ci in range(rows // SG_CHUNK):
        rs = slice(ci * SG_CHUNK, (ci + 1) * SG_CHUNK)
        parts = []
        for g in range(SG_GROUPS):
            hs = slice(g * HEAD_DIM, (g + 1) * HEAD_DIM)
            mixed = jnp.dot(ws_ref[g], v_ref[rs, hs], preferred_element_type=F32) + bs_ref[g]
            parts.append((u_ref[rs, hs].astype(F32) * mixed * gb_ref[rs, hs].astype(F32)).astype(BF16))
        row_parts.append(jnp.concatenate(parts, axis=1))
    y_b = jnp.concatenate(row_parts, axis=0)

    merged = (ma_ref[...].astype(F32) * jnp.dot(y_a, wa_ref[...], preferred_element_type=F32)
              + mb_ref[...].astype(F32) * jnp.dot(y_b, wb_ref[...], preferred_element_type=F32))
    x_new = x_ref[...] + jnp.dot(merged.astype(BF16), wo_ref[...], preferred_element_type=F32)
    xo_ref[...] = x_new
    y = x_new * lax.rsqrt(jnp.mean(x_new * x_new, axis=-1, keepdims=True) + RMS_EPS)
    ho_ref[...] = (y * nw_ref[...]).astype(ho_ref.dtype)


def _merge(o1, o2, gates, u, v, merges, x, gnorm_w, w_s, b_s, w_a, w_b, w_o, next_norm_w, h_dtype, block_rows=512):
    t = x.shape[0]

    def tok(colblock=0):
        return pl.BlockSpec((block_rows, D_MODEL), lambda i: (i, colblock))

    def whole(shape):
        return pl.BlockSpec(shape, lambda i: (0,) * len(shape))

    bs_full = jnp.broadcast_to(b_s.astype(F32)[:, :, None], (SG_GROUPS, SG_CHUNK, HEAD_DIM))
    return pl.pallas_call(
        _merge_kernel,
        grid=(t // block_rows,),
        in_specs=[tok(), tok(), tok(1), tok(), tok(), tok(2), tok(0), tok(1), tok(),
                  whole((1, HEAD_DIM)),
                  whole((SG_GROUPS, SG_CHUNK, SG_CHUNK)), whole((SG_GROUPS, SG_CHUNK, HEAD_DIM)),
                  whole((D_MODEL, D_MODEL)), whole((D_MODEL, D_MODEL)), whole((D_MODEL, D_MODEL)),
                  whole((1, D_MODEL))],
        out_specs=[tok(), tok()],
        out_shape=[jax.ShapeDtypeStruct((t, D_MODEL), F32), jax.ShapeDtypeStruct((t, D_MODEL), h_dtype)],
        compiler_params=_compiler_params(("parallel",)),
        name="merge",
    )(o1, o2, gates, u, v, gates, merges, merges, x,
      gnorm_w.reshape(1, HEAD_DIM), w_s.astype(BF16), bs_full, w_a, w_b, w_o, next_norm_w.reshape(1, D_MODEL))


def kernel(x, norm_w, w_in, lower_bounds, gnorm_w, ln_w, ln_b, w_s, b_s, w_proj_a, w_proj_b, w_out, final_norm_w):
    batch, seq, _ = x.shape
    t = batch * seq
    lb = _lower_bounds(lower_bounds)
    x2 = x.reshape(t, D_MODEL)
    h = _rmsnorm(x2, norm_w[0], BF16)
    for layer in range(DEPTH):
        last = layer == DEPTH - 1
        gates, k, l2f, val, u, v, merges = _in_proj(h, w_in[layer].astype(BF16), lb[layer], ln_w[layer], ln_b[layer])
        o1, o2 = _gla(gates, k, l2f, val, batch, seq)
        x2, h = _merge(o1.reshape(t, D_MODEL), o2.reshape(t, D_MODEL), gates, u, v, merges, x2,
                       gnorm_w[layer], w_s[layer], b_s[layer],
                       w_proj_a[layer].astype(BF16), w_proj_b[layer].astype(BF16), w_out[layer].astype(BF16),
                       final_norm_w if last else norm_w[layer + 1], F32 if last else BF16)
    return h.reshape(batch, seq, D_MODEL)
```

```python
import math

import jax
import jax.numpy as jnp
from jax import lax
from jax.experimental import pallas as pl
from jax.experimental.pallas import tpu as pltpu

D_MODEL = 1024
DEPTH = 4
HEAD_DIM = 128
N_HEADS = D_MODEL // HEAD_DIM
SG_CHUNK = 128
SG_GROUPS = D_MODEL // HEAD_DIM
RMS_EPS = 1e-6
LN_EPS = 1e-5
LB_FLOOR = 1e-20

COL_Q, COL_FF, COL_FB, COL_I, COL_GA, COL_U, COL_V, COL_GB, COL_MA, COL_MB = range(10)

REC_CHUNK = 128
LEVELS = (64, 32, 16, 8, 4, 2)
GLA_STEP_CHUNKS = 8

PROJ_BLOCK_ROWS = 2048
LOG2E = math.log2(math.e)
VMEM_LIMIT_BYTES = 56 * 1024 * 1024

F32 = jnp.float32
BF16 = jnp.bfloat16


def _compiler_params(semantics):
    return pltpu.CompilerParams(dimension_semantics=semantics, vmem_limit_bytes=VMEM_LIMIT_BYTES)


def _lower_bound_kernel(lb_ref, out_ref):
    x = lb_ref[...]
    m = jnp.max(x, axis=0, keepdims=True)
    e = jnp.exp(x - m)
    p = e / jnp.sum(e, axis=0, keepdims=True)
    acc = p[0:1]
    rows = [acc - p[0:1]]
    for layer in range(1, DEPTH):
        acc = acc + p[layer:layer + 1]
        rows.append(acc - p[0:1])
    out_ref[...] = jnp.concatenate(rows, axis=0)


def _lower_bounds(lower_bounds):
    flat = lower_bounds.astype(F32).reshape(DEPTH, 2 * D_MODEL)
    out = pl.pallas_call(
        _lower_bound_kernel,
        out_shape=jax.ShapeDtypeStruct((DEPTH, 2 * D_MODEL), F32),
        name="lower_bounds",
    )(flat)
    return out.reshape(DEPTH, 2, 1, D_MODEL)


def _rmsnorm_kernel(x_ref, w_ref, o_ref):
    x = x_ref[...]
    y = x * lax.rsqrt(jnp.mean(x * x, axis=-1, keepdims=True) + RMS_EPS)
    o_ref[...] = (y * w_ref[...]).astype(o_ref.dtype)


def _rmsnorm(x2d, w, out_dtype, block_rows=1024):
    t = x2d.shape[0]
    return pl.pallas_call(
        _rmsnorm_kernel,
        grid=(t // block_rows,),
        in_specs=[pl.BlockSpec((block_rows, D_MODEL), lambda i: (i, 0)),
                  pl.BlockSpec((1, D_MODEL), lambda i: (0, 0))],
        out_specs=pl.BlockSpec((block_rows, D_MODEL), lambda i: (i, 0)),
        out_shape=jax.ShapeDtypeStruct((t, D_MODEL), out_dtype),
        compiler_params=_compiler_params(("parallel",)),
        name="rmsnorm",
    )(x2d, w.reshape(1, D_MODEL))


def _sigmoid(x):
    return 1.0 / (1.0 + jnp.exp2(x * (-LOG2E)))


def _gelu(x):
    a = -2.0 * LOG2E * math.sqrt(2.0 / math.pi)
    t = (x * x) * (a * 0.044715) + a
    return x / (1.0 + jnp.exp2(x * t))


def _project_rows(h_ref, w_ref, wb_ref, emit, sub_rows):
    @pl.when(pl.program_id(1) == 0)
    def _():
        wb_ref[...] = w_ref[...].astype(wb_ref.dtype)

    blocks = [slice(r, r + sub_rows) for r in range(0, h_ref.shape[0], sub_rows)]
    acc = jnp.dot(h_ref[blocks[0], :], wb_ref[...], preferred_element_type=F32)
    for prev, rows in zip(blocks[:-1], blocks[1:]):
        nxt = jnp.dot(h_ref[rows, :], wb_ref[...], preferred_element_type=F32)
        emit(prev, acc)
        acc = nxt
    emit(blocks[-1], acc)


def _proj_plain_kernel(h_ref, w_ref, o_ref, wb_ref):
    def emit(rows, acc):
        o_ref[rows, :] = acc.astype(o_ref.dtype)
    _project_rows(h_ref, w_ref, wb_ref, emit, PROJ_BLOCK_ROWS)


def _proj_silu_kernel(h_ref, w_ref, scale_ref, o_ref, wb_ref):
    def emit(rows, acc):
        o_ref[rows, :] = (acc * _sigmoid(acc) * scale_ref[0]).astype(o_ref.dtype)
    _project_rows(h_ref, w_ref, wb_ref, emit, 256)


def _proj_sigmoid_kernel(h_ref, w_ref, o_ref, wb_ref):
    def emit(rows, acc):
        o_ref[rows, :] = _sigmoid(acc).astype(o_ref.dtype)
    _project_rows(h_ref, w_ref, wb_ref, emit, 128)


def _proj_gelu_kernel(h_ref, w_ref, o_ref, wb_ref):
    def emit(rows, acc):
        o_ref[rows, :] = _gelu(acc).astype(o_ref.dtype)
    _project_rows(h_ref, w_ref, wb_ref, emit, 128)


def _proj_gelu_ln_kernel(h_ref, w_ref, lnw_ref, lnb_ref, o_ref, wb_ref):
    def emit(rows, acc):
        v = _gelu(acc)
        vc = v - jnp.mean(v, axis=-1, keepdims=True)
        var = jnp.mean(vc * vc, axis=-1, keepdims=True)
        o_ref[rows, :] = (vc * lax.rsqrt(var + LN_EPS) * lnw_ref[...] + lnb_ref[...]).astype(o_ref.dtype)
    _project_rows(h_ref, w_ref, wb_ref, emit, 256)


def _proj_forget_kernel(h_ref, w_ref, lb_ref, k_ref, l2f_ref, wb_ref):
    lb = lb_ref[0]
    floor_lb, one_minus_lb = jnp.maximum(lb, LB_FLOOR), 1.0 - lb

    def emit(rows, acc):
        gated = one_minus_lb * _sigmoid(acc)
        k_ref[rows, :] = (one_minus_lb - gated).astype(k_ref.dtype)
        l2f_ref[rows, :] = jnp.log2(floor_lb + gated).astype(l2f_ref.dtype)
    _project_rows(h_ref, w_ref, wb_ref, emit, 128)


def _proj(kernel_fn, name, h, w, layer, col_of, n_cols, n_out=1, extra=()):
    t = h.shape[0]
    out_spec = pl.BlockSpec((PROJ_BLOCK_ROWS, D_MODEL), lambda j, i: (i, j))
    out_shape = jax.ShapeDtypeStruct((t, n_cols * D_MODEL), BF16)
    res = pl.pallas_call(
        kernel_fn,
        grid=(n_cols, t // PROJ_BLOCK_ROWS),
        in_specs=[pl.BlockSpec((PROJ_BLOCK_ROWS, D_MODEL), lambda j, i: (i, 0)),
                  pl.BlockSpec((None, D_MODEL, D_MODEL), lambda j, i: (layer, 0, col_of(j)))]
                 + [pl.BlockSpec(shape, imap) for _, shape, imap in extra],
        out_specs=[out_spec] * n_out,
        out_shape=[out_shape] * n_out,
        scratch_shapes=[pltpu.VMEM((D_MODEL, D_MODEL), BF16)],
        compiler_params=_compiler_params(("parallel", "arbitrary")),
        name=name,
    )(h, w, *[a for a, _, _ in extra])
    return res if n_out > 1 else res[0]


def _in_proj(h, w, layer, lb_layer, ln_w, ln_b):
    row = lambda a: a.reshape(1, D_MODEL)
    const = lambda j, i: (0, 0)
    scales = jnp.ones((3, 1, D_MODEL), F32).at[0].multiply(HEAD_DIM ** -0.5)
    gates = _proj(_proj_silu_kernel, "proj_silu", h, w, layer, lambda j: (j * 7 + 1) // 2, 3,
                  extra=[(scales, (1, 1, D_MODEL), lambda j, i: (j, 0, 0))])
    k, l2f = _proj(_proj_forget_kernel, "proj_forget", h, w, layer, lambda j: j + COL_FF, 2, n_out=2,
                   extra=[(lb_layer, (1, 1, D_MODEL), lambda j, i: (j, 0, 0))])
    val = _proj(_proj_plain_kernel, "proj_plain", h, w, layer, lambda j: j + COL_I, 1)
    u = _proj(_proj_gelu_kernel, "proj_gelu", h, w, layer, lambda j: j + COL_U, 1)
    v = _proj(_proj_gelu_ln_kernel, "proj_gelu_ln", h, w, layer, lambda j: j + COL_V, 1,
              extra=[(row(ln_w), (1, D_MODEL), const), (row(ln_b), (1, D_MODEL), const)])
    merges = _proj(_proj_sigmoid_kernel, "proj_sigmoid", h, w, layer, lambda j: j + COL_MA, 2)
    return gates, k, l2f, val, u, v, merges


def _dot_nt(a, b):
    return lax.dot_general(a, b, (((1,), (1,)), ((), ())), preferred_element_type=F32)


def _dot_tn(a, b):
    return lax.dot_general(a, b, (((0,), (0,)), ((), ())), preferred_element_type=F32)


def _boundary_rows(g, block, row_in_parent):
    c, width = g.shape
    parent = 2 * block
    g3 = g.reshape(c // 8, 8, width)
    sub = lax.broadcasted_iota(jnp.int32, g3.shape, 1)
    r = jnp.broadcast_to(g3[:, row_in_parent:row_in_parent + 1, :], g3.shape)
    for p in range(1, 8 // parent):
        row = p * parent + row_in_parent
        r = jnp.where(sub >= p * parent, jnp.broadcast_to(g3[:, row:row + 1, :], g3.shape), r)
    return r.reshape(c, width)


def _level_operands(q, kf, kb, gf, gb, block):
    c = q.shape[0]
    if block >= 8:
        eq_parts, ek_parts, k_parts = [], [], []
        for blk in range(c // block):
            rows = slice(blk * block, (blk + 1) * block)
            edge = (blk | 1) * block
            if blk % 2:
                eq_parts.append(gf[rows] - gf[edge - 1:edge])
                ek_parts.append(gb[edge:edge + 1] - gb[rows])
                k_parts.append(kb[rows])
            else:
                eq_parts.append(gb[rows] - gb[edge:edge + 1])
                ek_parts.append(gf[edge - 1:edge] - gf[rows])
                k_parts.append(kf[rows])
        q_l = q * jnp.exp2(jnp.concatenate(eq_parts, axis=0))
        k_l = jnp.concatenate(k_parts, axis=0) * jnp.exp2(jnp.concatenate(ek_parts, axis=0))
    else:
        d_f = gf - _boundary_rows(gf, block, block - 1)
        d_b = gb - _boundary_rows(gb, block, block)
        odd = (lax.broadcasted_iota(jnp.int32, q.shape, 0) & block) != 0
        q_l = q * jnp.exp2(jnp.minimum(d_f, d_b))
        k_l = jnp.where(odd, kb, kf) * jnp.exp2(-jnp.maximum(d_f, d_b))
    return q_l.astype(BF16), k_l.astype(BF16)


def _gla_kernel(q_ref, kf_ref, kb_ref, lf_ref, lb_ref, v_ref,
                q2_ref, kb2_ref, lb2_ref, v2_ref,
                o_ref, o2_ref, sf_ref, sb_ref):
    c = REC_CHUNK

    @pl.when(pl.program_id(1) == 0)
    def _():
        sf_ref[...] = jnp.zeros_like(sf_ref)
        sb_ref[...] = jnp.zeros_like(sb_ref)

    row = lax.broadcasted_iota(jnp.int32, (c, c), 0)
    col = lax.broadcasted_iota(jnp.int32, (c, c), 1)
    tri_lower = jnp.where(col <= row, 1.0, 0.0).astype(BF16)
    tri_upper = jnp.where(col >= row, 1.0, 0.0).astype(BF16)
    xor = row ^ col
    heads = [slice(h * HEAD_DIM, (h + 1) * HEAD_DIM) for h in range(N_HEADS)]

    def chunk(ci, carry):
        rows = pl.ds(pl.multiple_of(ci * c, c), c)
        rows2 = pl.ds(pl.multiple_of((GLA_STEP_CHUNKS - 1 - ci) * c, c), c)

        gb2 = jnp.dot(tri_upper, lb2_ref[0, rows2, :], preferred_element_type=F32)
        g_first = gb2[0:1, :]
        qdec2 = (q2_ref[0, rows2, :].astype(F32) * jnp.exp2(gb2)).astype(BF16)
        kdec2 = (kb2_ref[0, rows2, :].astype(F32) * jnp.exp2(g_first - gb2)).astype(BF16)
        decay2 = jnp.exp2(g_first)
        for h, hs in enumerate(heads):
            s_b = sb_ref[h]
            o2_ref[0, rows2, hs] = _dot_nt(qdec2[:, hs], s_b.astype(BF16)).astype(o2_ref.dtype)
            sb_ref[h] = s_b * decay2[:, hs] + _dot_tn(v2_ref[0, rows2, hs], kdec2[:, hs])

        q16 = q_ref[0, rows, :]
        q, kf, kb = q16.astype(F32), kf_ref[0, rows, :].astype(F32), kb_ref[0, rows, :].astype(F32)
        lf16, lb16 = lf_ref[0, rows, :], lb_ref[0, rows, :]
        gf = jnp.dot(tri_lower, lf16, preferred_element_type=F32)
        gb = jnp.dot(tri_upper, lb16, preferred_element_type=F32)

        attn = [None] * N_HEADS
        for block in LEVELS:
            q_l, k_l = _level_operands(q, kf, kb, gf, gb, block)
            for h, hs in enumerate(heads):
                p = _dot_nt(q_l[:, hs], k_l[:, hs])
                attn[h] = p if attn[h] is None else jnp.where(xor < 2 * block, p, attn[h])
        odd = (lax.broadcasted_iota(jnp.int32, q.shape, 0) & 1) != 0
        step = jnp.where(odd, lf16.astype(F32), lb16.astype(F32))
        q_l = (q * jnp.exp2(step)).astype(BF16)
        k_l = jnp.where(odd, kb, kf).astype(BF16)
        k_d = (kf + kb).astype(BF16)
        for h, hs in enumerate(heads):
            attn[h] = jnp.where(xor < 2, _dot_nt(q_l[:, hs], k_l[:, hs]), attn[h])
            attn[h] = jnp.where(xor < 1, _dot_nt(q16[:, hs], k_d[:, hs]), attn[h])

        g_last = gf[c - 1:c, :]
        qdec = (q * jnp.exp2(gf)).astype(BF16)
        kdec = (kf * jnp.exp2(g_last - gf)).astype(BF16)
        decay = jnp.exp2(g_last)
        for h, hs in enumerate(heads):
            s_f = sf_ref[h]
            v = v_ref[0, rows, hs]
            o = jnp.dot(attn[h].astype(BF16), v, preferred_element_type=F32)
            o_ref[0, rows, hs] = (o + _dot_nt(qdec[:, hs], s_f.astype(BF16))).astype(o_ref.dtype)
            sf_ref[h] = s_f * decay[:, hs] + _dot_tn(v, kdec[:, hs])
        return carry

    lax.fori_loop(0, GLA_STEP_CHUNKS, chunk, 0)


def _gla(gates, k, l2f, val, batch, seq):
    rows = GLA_STEP_CHUNKS * REC_CHUNK
    n = seq // rows
    as3 = lambda a: a.reshape(batch, seq, a.shape[-1])

    def fwd(colblock):
        return pl.BlockSpec((1, rows, D_MODEL), lambda bi, ni: (bi, ni, colblock))

    def bwd(colblock):
        return pl.BlockSpec((1, rows, D_MODEL), lambda bi, ni: (bi, n - 1 - ni, colblock))

    out_shape = jax.ShapeDtypeStruct((batch, seq, D_MODEL), BF16)
    g3, k3, l3, v3 = as3(gates), as3(k), as3(l2f), as3(val)
    return pl.pallas_call(
        _gla_kernel,
        grid=(batch, n),
        in_specs=[fwd(0), fwd(0), fwd(1), fwd(0), fwd(1), fwd(0),
                  bwd(0), bwd(1), bwd(1), bwd(0)],
        out_specs=[fwd(0), bwd(0)],
        out_shape=[out_shape, out_shape],
        scratch_shapes=[pltpu.VMEM((N_HEADS, HEAD_DIM, HEAD_DIM), F32),
                        pltpu.VMEM((N_HEADS, HEAD_DIM, HEAD_DIM), F32)],
        compiler_params=_compiler_params(("parallel", "arbitrary")),
        name="gla",
    )(g3, k3, k3, l3, l3, v3,
      g3, k3, l3, v3)


def _merge_kernel(o1_ref, o2_ref, ga_ref, u_ref, v_ref, gb_ref, ma_ref, mb_ref, x_ref,
                  gnw_ref, ws_ref, bs_ref, wa_ref, wb_ref, wo_ref, nw_ref,
                  xo_ref, ho_ref):
    rows = x_ref.shape[0]

    o = o1_ref[...].astype(F32) + o2_ref[...].astype(F32)
    parts = []
    for h in range(N_HEADS):
        hs = slice(h * HEAD_DIM, (h + 1) * HEAD_DIM)
        oh = o[:, hs]
        oh = oh * lax.rsqrt(jnp.mean(oh * oh, axis=-1, keepdims=True) + RMS_EPS) * gnw_ref[...]
        parts.append((oh * ga_ref[:, hs].astype(F32)).astype(BF16))
    y_a = jnp.concatenate(parts, axis=1)

    row_parts = []
    for ci in range(rows // SG_CHUNK):
        rs = slice(ci * SG_CHUNK, (ci + 1) * SG_CHUNK)
        parts = []
        for g in range(SG_GROUPS):
            hs = slice(g * HEAD_DIM, (g + 1) * HEAD_DIM)
            mixed = jnp.dot(ws_ref[g], v_ref[rs, hs], preferred_element_type=F32) + bs_ref[g]
            parts.append((u_ref[rs, hs].astype(F32) * mixed * gb_ref[rs, hs].astype(F32)).astype(BF16))
        row_parts.append(jnp.concatenate(parts, axis=1))
    y_b = jnp.concatenate(row_parts, axis=0)

    merged = (ma_ref[...].astype(F32) * jnp.dot(y_a, wa_ref[...], preferred_element_type=F32)
              + mb_ref[...].astype(F32) * jnp.dot(y_b, wb_ref[...], preferred_element_type=F32))
    x_new = x_ref[...] + jnp.dot(merged.astype(BF16), wo_ref[...], preferred_element_type=F32)
    xo_ref[...] = x_new
    y = x_new * lax.rsqrt(jnp.mean(x_new * x_new, axis=-1, keepdims=True) + RMS_EPS)
    ho_ref[...] = (y * nw_ref[...]).astype(ho_ref.dtype)


def _merge(o1, o2, gates, u, v, merges, x, gnorm_w, w_s, b_s, w_a, w_b, w_o, next_norm_w, h_dtype, block_rows=512):
    t = x.shape[0]

    def tok(colblock=0):
        return pl.BlockSpec((block_rows, D_MODEL), lambda i: (i, colblock))

    def whole(shape):
        return pl.BlockSpec(shape, lambda i: (0,) * len(shape))

    bs_full = jnp.broadcast_to(b_s.astype(F32)[:, :, None], (SG_GROUPS, SG_CHUNK, HEAD_DIM))
    return pl.pallas_call(
        _merge_kernel,
        grid=(t // block_rows,),
        in_specs=[tok(), tok(), tok(1), tok(), tok(), tok(2), tok(0), tok(1), tok(),
                  whole((1, HEAD_DIM)),
                  whole((SG_GROUPS, SG_CHUNK, SG_CHUNK)), whole((SG_GROUPS, SG_CHUNK, HEAD_DIM)),
                  whole((D_MODEL, D_MODEL)), whole((D_MODEL, D_MODEL)), whole((D_MODEL, D_MODEL)),
                  whole((1, D_MODEL))],
        out_specs=[tok(), tok()],
        out_shape=[jax.ShapeDtypeStruct((t, D_MODEL), F32), jax.ShapeDtypeStruct((t, D_MODEL), h_dtype)],
        compiler_params=_compiler_params(("parallel",)),
        name="merge",
    )(o1, o2, gates, u, v, gates, merges, merges, x,
      gnorm_w.reshape(1, HEAD_DIM), w_s.astype(BF16), bs_full, w_a, w_b, w_o, next_norm_w.reshape(1, D_MODEL))


def kernel(x, norm_w, w_in, lower_bounds, gnorm_w, ln_w, ln_b, w_s, b_s, w_proj_a, w_proj_b, w_out, final_norm_w):
    batch, seq, _ = x.shape
    t = batch * seq
    lb = _lower_bounds(lower_bounds)
    x2 = x.reshape(t, D_MODEL)
    h = _rmsnorm(x2, norm_w[0], BF16)
    for layer in range(DEPTH):
        last = layer == DEPTH - 1
        gates, k, l2f, val, u, v, merges = _in_proj(h, w_in, layer, lb[layer], ln_w[layer], ln_b[layer])
        o1, o2 = _gla(gates, k, l2f, val, batch, seq)
        x2, h = _merge(o1.reshape(t, D_MODEL), o2.reshape(t, D_MODEL), gates, u, v, merges, x2,
                       gnorm_w[layer], w_s[layer], b_s[layer],
                       w_proj_a[layer].astype(BF16), w_proj_b[layer].astype(BF16), w_out[layer].astype(BF16),
                       final_norm_w if last else norm_w[layer + 1], F32 if last else BF16)
    return h.reshape(batch, seq, D_MODEL)
```

```python
import math

import jax
import jax.numpy as jnp
from jax import lax
from jax.experimental import pallas as pl
from jax.experimental.pallas import tpu as pltpu

D_MODEL = 1024
DEPTH = 4
HEAD_DIM = 128
N_HEADS = D_MODEL // HEAD_DIM
SG_CHUNK = 128
SG_GROUPS = D_MODEL // HEAD_DIM
RMS_EPS = 1e-6
LN_EPS = 1e-5
LB_FLOOR = 1e-20

COL_Q, COL_FF, COL_FB, COL_I, COL_GA, COL_U, COL_V, COL_GB, COL_MA, COL_MB = range(10)

REC_CHUNK = 128
LEVELS = (64, 32, 16, 8, 4, 2)
GLA_STEP_CHUNKS = 8

PROJ_BLOCK_ROWS = 4096
LOG2E = math.log2(math.e)
VMEM_LIMIT_BYTES = 56 * 1024 * 1024

F32 = jnp.float32
BF16 = jnp.bfloat16


def _compiler_params(semantics):
    return pltpu.CompilerParams(dimension_semantics=semantics, vmem_limit_bytes=VMEM_LIMIT_BYTES)


def _lower_bound_kernel(lb_ref, out_ref):
    x = lb_ref[...]
    m = jnp.max(x, axis=0, keepdims=True)
    e = jnp.exp(x - m)
    p = e / jnp.sum(e, axis=0, keepdims=True)
    acc = p[0:1]
    rows = [acc - p[0:1]]
    for layer in range(1, DEPTH):
        acc = acc + p[layer:layer + 1]
        rows.append(acc - p[0:1])
    out_ref[...] = jnp.concatenate(rows, axis=0)


def _lower_bounds(lower_bounds):
    flat = lower_bounds.astype(F32).reshape(DEPTH, 2 * D_MODEL)
    out = pl.pallas_call(
        _lower_bound_kernel,
        out_shape=jax.ShapeDtypeStruct((DEPTH, 2 * D_MODEL), F32),
        name="lower_bounds",
    )(flat)
    return out.reshape(DEPTH, 2, 1, D_MODEL)


def _rmsnorm_kernel(x_ref, w_ref, o_ref):
    x = x_ref[...]
    y = x * lax.rsqrt(jnp.mean(x * x, axis=-1, keepdims=True) + RMS_EPS)
    o_ref[...] = (y * w_ref[...]).astype(o_ref.dtype)


def _rmsnorm(x2d, w, out_dtype, block_rows=2048):
    t = x2d.shape[0]
    return pl.pallas_call(
        _rmsnorm_kernel,
        grid=(t // block_rows,),
        in_specs=[pl.BlockSpec((block_rows, D_MODEL), lambda i: (i, 0)),
                  pl.BlockSpec((1, D_MODEL), lambda i: (0, 0))],
        out_specs=pl.BlockSpec((block_rows, D_MODEL), lambda i: (i, 0)),
        out_shape=jax.ShapeDtypeStruct((t, D_MODEL), out_dtype),
        compiler_params=_compiler_params(("parallel",)),
        name="rmsnorm",
    )(x2d, w.reshape(1, D_MODEL))


def _sigmoid(x):
    return 1.0 / (1.0 + jnp.exp2(x * (-LOG2E)))


def _gelu(x):
    a = -2.0 * LOG2E * math.sqrt(2.0 / math.pi)
    t = (x * x) * (a * 0.044715) + a
    return x / (1.0 + jnp.exp2(x * t))


def _project_rows(h_ref, w_ref, wb_ref, emit, sub_rows):
    @pl.when(pl.program_id(1) == 0)
    def _():
        wb_ref[...] = w_ref[...].astype(wb_ref.dtype)

    blocks = [slice(r, r + sub_rows) for r in range(0, h_ref.shape[0], sub_rows)]
    acc = jnp.dot(h_ref[blocks[0], :], wb_ref[...], preferred_element_type=F32)
    for prev, rows in zip(blocks[:-1], blocks[1:]):
        nxt = jnp.dot(h_ref[rows, :], wb_ref[...], preferred_element_type=F32)
        emit(prev, acc)
        acc = nxt
    emit(blocks[-1], acc)


def _proj_plain_kernel(h_ref, w_ref, o_ref, wb_ref):
    def emit(rows, acc):
        o_ref[rows, :] = acc.astype(o_ref.dtype)
    _project_rows(h_ref, w_ref, wb_ref, emit, 2048)


def _proj_silu_kernel(h_ref, w_ref, scale_ref, o_ref, wb_ref):
    def emit(rows, acc):
        o_ref[rows, :] = (acc * _sigmoid(acc) * scale_ref[0]).astype(o_ref.dtype)
    _project_rows(h_ref, w_ref, wb_ref, emit, 256)


def _proj_sigmoid_kernel(h_ref, w_ref, o_ref, wb_ref):
    def emit(rows, acc):
        o_ref[rows, :] = _sigmoid(acc).astype(o_ref.dtype)
    _project_rows(h_ref, w_ref, wb_ref, emit, 128)


def _proj_gelu_ln_kernel(h_ref, w_ref, lnw_ref, lnb_ref, o_ref, wb_ref):
    def emit(rows, acc):
        v = _gelu(acc)
        vc = v - jnp.mean(v, axis=-1, keepdims=True)
        var = jnp.mean(vc * vc, axis=-1, keepdims=True)
        o_ref[rows, :] = (vc * lax.rsqrt(var + LN_EPS) * lnw_ref[...] + lnb_ref[...]).astype(o_ref.dtype)
    _project_rows(h_ref, w_ref, wb_ref, emit, 256)


def _proj_forget_kernel(h_ref, w_ref, lb_ref, k_ref, l2f_ref, wb_ref):
    lb = lb_ref[0]
    floor_lb, one_minus_lb = jnp.maximum(lb, LB_FLOOR), 1.0 - lb

    def emit(rows, acc):
        gated = one_minus_lb * _sigmoid(acc)
        k_ref[rows, :] = (one_minus_lb - gated).astype(k_ref.dtype)
        l2f_ref[rows, :] = jnp.log2(floor_lb + gated).astype(l2f_ref.dtype)
    _project_rows(h_ref, w_ref, wb_ref, emit, 128)


def _proj(kernel_fn, name, h, w, layer, col_of, n_cols, n_out=1, extra=()):
    t = h.shape[0]
    rows = PROJ_BLOCK_ROWS // n_out
    out_spec = pl.BlockSpec((rows, D_MODEL), lambda j, i: (i, j))
    out_shape = jax.ShapeDtypeStruct((t, n_cols * D_MODEL), BF16)
    res = pl.pallas_call(
        kernel_fn,
        grid=(n_cols, t // rows),
        in_specs=[pl.BlockSpec((rows, D_MODEL), lambda j, i: (i, 0)),
                  pl.BlockSpec((None, D_MODEL, D_MODEL), lambda j, i: (layer, 0, col_of(j)))]
                 + [pl.BlockSpec(shape, imap) for _, shape, imap in extra],
        out_specs=[out_spec] * n_out,
        out_shape=[out_shape] * n_out,
        scratch_shapes=[pltpu.VMEM((D_MODEL, D_MODEL), BF16)],
        compiler_params=_compiler_params(("parallel", "arbitrary")),
        name=name,
    )(h, w, *[a for a, _, _ in extra])
    return res if n_out > 1 else res[0]


def _in_proj(h, w, layer, lb_layer, ln_w, ln_b):
    row = lambda a: a.reshape(1, D_MODEL)
    const = lambda j, i: (0, 0)
    scales = jnp.ones((3, 1, D_MODEL), F32).at[0].multiply(HEAD_DIM ** -0.5)
    gates = _proj(_proj_silu_kernel, "proj_silu", h, w, layer, lambda j: (j * 7 + 1) // 2, 3,
                  extra=[(scales, (1, 1, D_MODEL), lambda j, i: (j, 0, 0))])
    k, l2f = _proj(_proj_forget_kernel, "proj_forget", h, w, layer, lambda j: j + COL_FF, 2, n_out=2,
                   extra=[(lb_layer, (1, 1, D_MODEL), lambda j, i: (j, 0, 0))])
    plain = _proj(_proj_plain_kernel, "proj_plain", h, w, layer, lambda j: COL_I + 2 * j, 2)
    v = _proj(_proj_gelu_ln_kernel, "proj_gelu_ln", h, w, layer, lambda j: j + COL_V, 1,
              extra=[(row(ln_w), (1, D_MODEL), const), (row(ln_b), (1, D_MODEL), const)])
    merges = _proj(_proj_sigmoid_kernel, "proj_sigmoid", h, w, layer, lambda j: j + COL_MA, 2)
    return gates, k, l2f, plain, v, merges


def _dot_nt(a, b):
    return lax.dot_general(a, b, (((1,), (1,)), ((), ())), preferred_element_type=F32)


def _dot_tn(a, b):
    return lax.dot_general(a, b, (((0,), (0,)), ((), ())), preferred_element_type=F32)


def _boundary_rows(g, block, row_in_parent):
    c, width = g.shape
    parent = 2 * block
    g3 = g.reshape(c // 8, 8, width)
    sub = lax.broadcasted_iota(jnp.int32, g3.shape, 1)
    r = jnp.broadcast_to(g3[:, row_in_parent:row_in_parent + 1, :], g3.shape)
    for p in range(1, 8 // parent):
        row = p * parent + row_in_parent
        r = jnp.where(sub >= p * parent, jnp.broadcast_to(g3[:, row:row + 1, :], g3.shape), r)
    return r.reshape(c, width)


def _level_operands(q, kf, kb, gf, gb, block):
    c = q.shape[0]
    if block >= 8:
        eq_parts, ek_parts, k_parts = [], [], []
        for blk in range(c // block):
            rows = slice(blk * block, (blk + 1) * block)
            edge = (blk | 1) * block
            if blk % 2:
                eq_parts.append(gf[rows] - gf[edge - 1:edge])
                ek_parts.append(gb[edge:edge + 1] - gb[rows])
                k_parts.append(kb[rows])
            else:
                eq_parts.append(gb[rows] - gb[edge:edge + 1])
                ek_parts.append(gf[edge - 1:edge] - gf[rows])
                k_parts.append(kf[rows])
        q_l = q * jnp.exp2(jnp.concatenate(eq_parts, axis=0))
        k_l = jnp.concatenate(k_parts, axis=0) * jnp.exp2(jnp.concatenate(ek_parts, axis=0))
    else:
        d_f = gf - _boundary_rows(gf, block, block - 1)
        d_b = gb - _boundary_rows(gb, block, block)
        odd = (lax.broadcasted_iota(jnp.int32, q.shape, 0) & block) != 0
        q_l = q * jnp.exp2(jnp.minimum(d_f, d_b))
        k_l = jnp.where(odd, kb, kf) * jnp.exp2(-jnp.maximum(d_f, d_b))
    return q_l.astype(BF16), k_l.astype(BF16)


def _gla_kernel(q_ref, kf_ref, kb_ref, lf_ref, lb_ref, v_ref,
                q2_ref, kb2_ref, lb2_ref, v2_ref,
                o_ref, o2_ref, sf_ref, sb_ref):
    c = REC_CHUNK

    @pl.when(pl.program_id(1) == 0)
    def _():
        sf_ref[...] = jnp.zeros_like(sf_ref)
        sb_ref[...] = jnp.zeros_like(sb_ref)

    row = lax.broadcasted_iota(jnp.int32, (c, c), 0)
    col = lax.broadcasted_iota(jnp.int32, (c, c), 1)
    tri_lower = jnp.where(col <= row, 1.0, 0.0).astype(BF16)
    tri_upper = jnp.where(col >= row, 1.0, 0.0).astype(BF16)
    xor = row ^ col
    heads = [slice(h * HEAD_DIM, (h + 1) * HEAD_DIM) for h in range(N_HEADS)]

    def chunk(ci, carry):
        rows = pl.ds(pl.multiple_of(ci * c, c), c)
        rows2 = pl.ds(pl.multiple_of((GLA_STEP_CHUNKS - 1 - ci) * c, c), c)

        gb2 = jnp.dot(tri_upper, lb2_ref[0, rows2, :], preferred_element_type=F32)
        g_first = gb2[0:1, :]
        qdec2 = (q2_ref[0, rows2, :].astype(F32) * jnp.exp2(gb2)).astype(BF16)
        kdec2 = (kb2_ref[0, rows2, :].astype(F32) * jnp.exp2(g_first - gb2)).astype(BF16)
        decay2 = jnp.exp2(g_first)
        for h, hs in enumerate(heads):
            s_b = sb_ref[h]
            o2_ref[0, rows2, hs] = _dot_nt(qdec2[:, hs], s_b.astype(BF16)).astype(o2_ref.dtype)
            sb_ref[h] = s_b * decay2[:, hs] + _dot_tn(v2_ref[0, rows2, hs], kdec2[:, hs])

        q16 = q_ref[0, rows, :]
        q, kf, kb = q16.astype(F32), kf_ref[0, rows, :].astype(F32), kb_ref[0, rows, :].astype(F32)
        lf16, lb16 = lf_ref[0, rows, :], lb_ref[0, rows, :]
        gf = jnp.dot(tri_lower, lf16, preferred_element_type=F32)
        gb = jnp.dot(tri_upper, lb16, preferred_element_type=F32)

        attn = [None] * N_HEADS
        for block in LEVELS:
            q_l, k_l = _level_operands(q, kf, kb, gf, gb, block)
            for h, hs in enumerate(heads):
                p = _dot_nt(q_l[:, hs], k_l[:, hs])
                attn[h] = p if attn[h] is None else jnp.where(xor < 2 * block, p, attn[h])
        odd = (lax.broadcasted_iota(jnp.int32, q.shape, 0) & 1) != 0
        step = jnp.where(odd, lf16.astype(F32), lb16.astype(F32))
        q_l = (q * jnp.exp2(step)).astype(BF16)
        k_l = jnp.where(odd, kb, kf).astype(BF16)
        k_d = (kf + kb).astype(BF16)
        for h, hs in enumerate(heads):
            attn[h] = jnp.where(xor < 2, _dot_nt(q_l[:, hs], k_l[:, hs]), attn[h])
            attn[h] = jnp.where(xor < 1, _dot_nt(q16[:, hs], k_d[:, hs]), attn[h])

        g_last = gf[c - 1:c, :]
        qdec = (q * jnp.exp2(gf)).astype(BF16)
        kdec = (kf * jnp.exp2(g_last - gf)).astype(BF16)
        decay = jnp.exp2(g_last)
        for h, hs in enumerate(heads):
            s_f = sf_ref[h]
            v = v_ref[0, rows, hs]
            o = jnp.dot(attn[h].astype(BF16), v, preferred_element_type=F32)
            o_ref[0, rows, hs] = (o + _dot_nt(qdec[:, hs], s_f.astype(BF16))).astype(o_ref.dtype)
            sf_ref[h] = s_f * decay[:, hs] + _dot_tn(v, kdec[:, hs])
        return carry

    lax.fori_loop(0, GLA_STEP_CHUNKS, chunk, 0)


def _gla(gates, k, l2f, val, batch, seq):
    rows = GLA_STEP_CHUNKS * REC_CHUNK
    n = seq // rows
    as3 = lambda a: a.reshape(batch, seq, a.shape[-1])

    def fwd(colblock):
        return pl.BlockSpec((1, rows, D_MODEL), lambda bi, ni: (bi, ni, colblock))

    def bwd(colblock):
        return pl.BlockSpec((1, rows, D_MODEL), lambda bi, ni: (bi, n - 1 - ni, colblock))

    out_shape = jax.ShapeDtypeStruct((batch, seq, D_MODEL), BF16)
    g3, k3, l3, v3 = as3(gates), as3(k), as3(l2f), as3(val)
    return pl.pallas_call(
        _gla_kernel,
        grid=(batch, n),
        in_specs=[fwd(0), fwd(0), fwd(1), fwd(0), fwd(1), fwd(0),
                  bwd(0), bwd(1), bwd(1), bwd(0)],
        out_specs=[fwd(0), bwd(0)],
        out_shape=[out_shape, out_shape],
        scratch_shapes=[pltpu.VMEM((N_HEADS, HEAD_DIM, HEAD_DIM), F32),
                        pltpu.VMEM((N_HEADS, HEAD_DIM, HEAD_DIM), F32)],
        compiler_params=_compiler_params(("parallel", "arbitrary")),
        name="gla",
    )(g3, k3, k3, l3, l3, v3,
      g3, k3, l3, v3)


def _merge_kernel(o1_ref, o2_ref, ga_ref, u_ref, v_ref, gb_ref, ma_ref, mb_ref, x_ref,
                  gnw_ref, ws_ref, bs_ref, wa_ref, wb_ref, wo_ref, nw_ref,
                  xo_ref, ho_ref):
    rows = x_ref.shape[0]

    o = o1_ref[...].astype(F32) + o2_ref[...].astype(F32)
    parts = []
    for h in range(N_HEADS):
        hs = slice(h * HEAD_DIM, (h + 1) * HEAD_DIM)
        oh = o[:, hs]
        oh = oh * lax.rsqrt(jnp.mean(oh * oh, axis=-1, keepdims=True) + RMS_EPS) * gnw_ref[...]
        parts.append((oh * ga_ref[:, hs].astype(F32)).astype(BF16))
    y_a = jnp.concatenate(parts, axis=1)

    row_parts = []
    for ci in range(rows // SG_CHUNK):
        rs = slice(ci * SG_CHUNK, (ci + 1) * SG_CHUNK)
        parts = []
        for g in range(SG_GROUPS):
            hs = slice(g * HEAD_DIM, (g + 1) * HEAD_DIM)
            mixed = jnp.dot(ws_ref[g], v_ref[rs, hs], preferred_element_type=F32) + bs_ref[g]
            parts.append((_gelu(u_ref[rs, hs].astype(F32)) * mixed * gb_ref[rs, hs].astype(F32)).astype(BF16))
        row_parts.append(jnp.concatenate(parts, axis=1))
    y_b = jnp.concatenate(row_parts, axis=0)

    merged = (ma_ref[...].astype(F32) * jnp.dot(y_a, wa_ref[...], preferred_element_type=F32)
              + mb_ref[...].astype(F32) * jnp.dot(y_b, wb_ref[...], preferred_element_type=F32))
    x_new = x_ref[...] + jnp.dot(merged.astype(BF16), wo_ref[...], preferred_element_type=F32)
    xo_ref[...] = x_new
    y = x_new * lax.rsqrt(jnp.mean(x_new * x_new, axis=-1, keepdims=True) + RMS_EPS)
    ho_ref[...] = (y * nw_ref[...]).astype(ho_ref.dtype)


def _merge(o1, o2, gates, plain, v, merges, x, gnorm_w, w_s, b_s, w_a, w_b, w_o, next_norm_w, h_dtype, block_rows=512):
    t = x.shape[0]

    def tok(colblock=0):
        return pl.BlockSpec((block_rows, D_MODEL), lambda i: (i, colblock))

    def whole(shape):
        return pl.BlockSpec(shape, lambda i: (0,) * len(shape))

    bs_full = jnp.broadcast_to(b_s.astype(F32)[:, :, None], (SG_GROUPS, SG_CHUNK, HEAD_DIM))
    return pl.pallas_call(
        _merge_kernel,
        grid=(t // block_rows,),
        in_specs=[tok(), tok(), tok(1), tok(1), tok(), tok(2), tok(0), tok(1), tok(),
                  whole((1, HEAD_DIM)),
                  whole((SG_GROUPS, SG_CHUNK, SG_CHUNK)), whole((SG_GROUPS, SG_CHUNK, HEAD_DIM)),
                  whole((D_MODEL, D_MODEL)), whole((D_MODEL, D_MODEL)), whole((D_MODEL, D_MODEL)),
                  whole((1, D_MODEL))],
        out_specs=[tok(), tok()],
        out_shape=[jax.ShapeDtypeStruct((t, D_MODEL), F32), jax.ShapeDtypeStruct((t, D_MODEL), h_dtype)],
        compiler_params=_compiler_params(("parallel",)),
        name="merge",
    )(o1, o2, gates, plain, v, gates, merges, merges, x,
      gnorm_w.reshape(1, HEAD_DIM), w_s.astype(BF16), bs_full, w_a, w_b, w_o, next_norm_w.reshape(1, D_MODEL))


def kernel(x, norm_w, w_in, lower_bounds, gnorm_w, ln_w, ln_b, w_s, b_s, w_proj_a, w_proj_b, w_out, final_norm_w):
    batch, seq, _ = x.shape
    t = batch * seq
    lb = _lower_bounds(lower_bounds)
    x2 = x.reshape(t, D_MODEL)
    h = _rmsnorm(x2, norm_w[0], BF16)
    for layer in range(DEPTH):
        last = layer == DEPTH - 1
        gates, k, l2f, plain, v, merges = _in_proj(h, w_in, layer, lb[layer], ln_w[layer], ln_b[layer])
        o1, o2 = _gla(gates, k, l2f, plain, batch, seq)
        x2, h = _merge(o1.reshape(t, D_MODEL), o2.reshape(t, D_MODEL), gates, plain, v, merges, x2,
                       gnorm_w[layer], w_s[layer], b_s[layer],
                       w_proj_a[layer].astype(BF16), w_proj_b[layer].astype(BF16), w_out[layer].astype(BF16),
                       final_norm_w if last else norm_w[layer + 1], F32 if last else BF16)
    return h.reshape(batch, seq, D_MODEL)
```

```python
import math

import jax
import jax.numpy as jnp
from jax import lax
from jax.experimental import pallas as pl
from jax.experimental.pallas import tpu as pltpu

D_MODEL = 1024
DEPTH = 4
HEAD_DIM = 128
N_HEADS = D_MODEL // HEAD_DIM
SG_CHUNK = 128
SG_GROUPS = D_MODEL // HEAD_DIM
RMS_EPS = 1e-6
LN_EPS = 1e-5
LB_FLOOR = 1e-20

COL_Q, COL_FF, COL_FB, COL_I, COL_GA, COL_U, COL_V, COL_GB, COL_MA, COL_MB = range(10)

REC_CHUNK = 128
LEVELS = (64, 32, 16, 8, 4, 2)
GLA_STEP_CHUNKS = 8

PROJ_BLOCK_ROWS = 4096
LOG2E = math.log2(math.e)
VMEM_LIMIT_BYTES = 56 * 1024 * 1024

F32 = jnp.float32
BF16 = jnp.bfloat16


def _compiler_params(semantics):
    return pltpu.CompilerParams(dimension_semantics=semantics, vmem_limit_bytes=VMEM_LIMIT_BYTES)


def _lower_bound_kernel(lb_ref, out_ref):
    x = lb_ref[...]
    m = jnp.max(x, axis=0, keepdims=True)
    e = jnp.exp(x - m)
    p = e / jnp.sum(e, axis=0, keepdims=True)
    acc = p[0:1]
    rows = [acc - p[0:1]]
    for layer in range(1, DEPTH):
        acc = acc + p[layer:layer + 1]
        rows.append(acc - p[0:1])
    out_ref[...] = jnp.concatenate(rows, axis=0)


def _lower_bounds(lower_bounds):
    flat = lower_bounds.astype(F32).reshape(DEPTH, 2 * D_MODEL)
    out = pl.pallas_call(
        _lower_bound_kernel,
        out_shape=jax.ShapeDtypeStruct((DEPTH, 2 * D_MODEL), F32),
        name="lower_bounds",
    )(flat)
    return out.reshape(DEPTH, 2, 1, D_MODEL)


def _rmsnorm_kernel(x_ref, w_ref, o_ref):
    x = x_ref[...]
    y = x * lax.rsqrt(jnp.mean(x * x, axis=-1, keepdims=True) + RMS_EPS)
    o_ref[...] = (y * w_ref[...]).astype(o_ref.dtype)


def _rmsnorm(x2d, w, out_dtype, block_rows=2048):
    t = x2d.shape[0]
    return pl.pallas_call(
        _rmsnorm_kernel,
        grid=(t // block_rows,),
        in_specs=[pl.BlockSpec((block_rows, D_MODEL), lambda i: (i, 0)),
                  pl.BlockSpec((1, D_MODEL), lambda i: (0, 0))],
        out_specs=pl.BlockSpec((block_rows, D_MODEL), lambda i: (i, 0)),
        out_shape=jax.ShapeDtypeStruct((t, D_MODEL), out_dtype),
        compiler_params=_compiler_params(("parallel",)),
        name="rmsnorm",
    )(x2d, w.reshape(1, D_MODEL))


def _sigmoid(x):
    return 1.0 / (1.0 + jnp.exp2(x * (-LOG2E)))


def _gelu(x):
    a = -2.0 * LOG2E * math.sqrt(2.0 / math.pi)
    t = (x * x) * (a * 0.044715) + a
    return x / (1.0 + jnp.exp2(x * t))


def _project_rows(h_ref, w_ref, wb_ref, emit, sub_rows):
    @pl.when(pl.program_id(1) == 0)
    def _():
        wb_ref[...] = w_ref[...].astype(wb_ref.dtype)

    blocks = [slice(r, r + sub_rows) for r in range(0, h_ref.shape[0], sub_rows)]
    acc = jnp.dot(h_ref[blocks[0], :], wb_ref[...], preferred_element_type=F32)
    for prev, rows in zip(blocks[:-1], blocks[1:]):
        nxt = jnp.dot(h_ref[rows, :], wb_ref[...], preferred_element_type=F32)
        emit(prev, acc)
        acc = nxt
    emit(blocks[-1], acc)


def _proj_plain_kernel(h_ref, w_ref, o_ref, wb_ref):
    def emit(rows, acc):
        o_ref[rows, :] = acc.astype(o_ref.dtype)
    _project_rows(h_ref, w_ref, wb_ref, emit, 2048)


def _proj_silu_kernel(h_ref, w_ref, scale_ref, o_ref, wb_ref):
    def emit(rows, acc):
        o_ref[rows, :] = (acc * _sigmoid(acc) * scale_ref[0]).astype(o_ref.dtype)
    _project_rows(h_ref, w_ref, wb_ref, emit, 256)


def _proj_sigmoid_kernel(h_ref, w_ref, o_ref, wb_ref):
    def emit(rows, acc):
        o_ref[rows, :] = _sigmoid(acc).astype(o_ref.dtype)
    _project_rows(h_ref, w_ref, wb_ref, emit, 128)


def _proj_gelu_ln_kernel(h_ref, w_ref, lnw_ref, lnb_ref, o_ref, wb_ref):
    def emit(rows, acc):
        v = _gelu(acc)
        vc = v - jnp.mean(v, axis=-1, keepdims=True)
        var = jnp.mean(vc * vc, axis=-1, keepdims=True)
        o_ref[rows, :] = (vc * lax.rsqrt(var + LN_EPS) * lnw_ref[...] + lnb_ref[...]).astype(o_ref.dtype)
    _project_rows(h_ref, w_ref, wb_ref, emit, 256)


def _proj_forget_kernel(h_ref, w_ref, lb_ref, k_ref, l2f_ref, wb_ref):
    lb = lb_ref[0]
    floor_lb, one_minus_lb = jnp.maximum(lb, LB_FLOOR), 1.0 - lb

    def emit(rows, acc):
        gated = one_minus_lb * _sigmoid(acc)
        k_ref[rows, :] = (one_minus_lb - gated).astype(k_ref.dtype)
        l2f_ref[rows, :] = jnp.log2(floor_lb + gated).astype(l2f_ref.dtype)
    _project_rows(h_ref, w_ref, wb_ref, emit, 128)


def _proj(kernel_fn, name, h, w, layer, col_of, n_cols, n_out=1, extra=()):
    t = h.shape[0]
    rows = PROJ_BLOCK_ROWS // n_out
    out_spec = pl.BlockSpec((rows, D_MODEL), lambda j, i: (i, j))
    out_shape = jax.ShapeDtypeStruct((t, n_cols * D_MODEL), BF16)
    res = pl.pallas_call(
        kernel_fn,
        grid=(n_cols, t // rows),
        in_specs=[pl.BlockSpec((rows, D_MODEL), lambda j, i: (i, 0)),
                  pl.BlockSpec((None, D_MODEL, D_MODEL), lambda j, i: (layer, 0, col_of(j)))]
                 + [pl.BlockSpec(shape, imap) for _, shape, imap in extra],
        out_specs=[out_spec] * n_out,
        out_shape=[out_shape] * n_out,
        scratch_shapes=[pltpu.VMEM((D_MODEL, D_MODEL), BF16)],
        compiler_params=_compiler_params(("parallel", "arbitrary")),
        name=name,
    )(h, w, *[a for a, _, _ in extra])
    return res if n_out > 1 else res[0]


def _in_proj(h, w, layer, lb_layer, ln_w, ln_b):
    row = lambda a: a.reshape(1, D_MODEL)
    const = lambda j, i: (0, 0)
    scales = jnp.ones((3, 1, D_MODEL), F32).at[0].multiply(HEAD_DIM ** -0.5)
    gates = _proj(_proj_silu_kernel, "proj_silu", h, w, layer, lambda j: (j * 7 + 1) // 2, 3,
                  extra=[(scales, (1, 1, D_MODEL), lambda j, i: (j, 0, 0))])
    k, l2f = _proj(_proj_forget_kernel, "proj_forget", h, w, layer, lambda j: j + COL_FF, 2, n_out=2,
                   extra=[(lb_layer, (1, 1, D_MODEL), lambda j, i: (j, 0, 0))])
    plain = _proj(_proj_plain_kernel, "proj_plain", h, w, layer, lambda j: COL_I + 2 * j, 2)
    v = _proj(_proj_gelu_ln_kernel, "proj_gelu_ln", h, w, layer, lambda j: j + COL_V, 1,
              extra=[(row(ln_w), (1, D_MODEL), const), (row(ln_b), (1, D_MODEL), const)])
    merges = _proj(_proj_sigmoid_kernel, "proj_sigmoid", h, w, layer, lambda j: j + COL_MA, 2)
    return gates, k, l2f, plain, v, merges


def _dot_nt(a, b):
    return lax.dot_general(a, b, (((1,), (1,)), ((), ())), preferred_element_type=F32)


def _dot_tn(a, b):
    return lax.dot_general(a, b, (((0,), (0,)), ((), ())), preferred_element_type=F32)


def _boundary_rows(g, block, row_in_parent):
    c, width = g.shape
    parent = 2 * block
    g3 = g.reshape(c // 8, 8, width)
    sub = lax.broadcasted_iota(jnp.int32, g3.shape, 1)
    r = jnp.broadcast_to(g3[:, row_in_parent:row_in_parent + 1, :], g3.shape)
    for p in range(1, 8 // parent):
        row = p * parent + row_in_parent
        r = jnp.where(sub >= p * parent, jnp.broadcast_to(g3[:, row:row + 1, :], g3.shape), r)
    return r.reshape(c, width)


def _level_operands(q, kf, kb, gf, gb, block):
    c = q.shape[0]
    if block >= 8:
        eq_parts, ek_parts, k_parts = [], [], []
        for blk in range(c // block):
            rows = slice(blk * block, (blk + 1) * block)
            edge = (blk | 1) * block
            if blk % 2:
                eq_parts.append(gf[rows] - gf[edge - 1:edge])
                ek_parts.append(gb[edge:edge + 1] - gb[rows])
                k_parts.append(kb[rows])
            else:
                eq_parts.append(gb[rows] - gb[edge:edge + 1])
                ek_parts.append(gf[edge - 1:edge] - gf[rows])
                k_parts.append(kf[rows])
        q_l = q * jnp.exp2(jnp.concatenate(eq_parts, axis=0))
        k_l = jnp.concatenate(k_parts, axis=0) * jnp.exp2(jnp.concatenate(ek_parts, axis=0))
    else:
        d_f = gf - _boundary_rows(gf, block, block - 1)
        d_b = gb - _boundary_rows(gb, block, block)
        odd = (lax.broadcasted_iota(jnp.int32, q.shape, 0) & block) != 0
        q_l = q * jnp.exp2(jnp.minimum(d_f, d_b))
        k_l = jnp.where(odd, kb, kf) * jnp.exp2(-jnp.maximum(d_f, d_b))
    return q_l.astype(BF16), k_l.astype(BF16)


def _gla_kernel(q_ref, kf_ref, kb_ref, lf_ref, lb_ref, v_ref,
                q2_ref, kb2_ref, lb2_ref, v2_ref,
                o_ref, o2_ref, sf_ref, sb_ref):
    c = REC_CHUNK

    @pl.when(pl.program_id(1) == 0)
    def _():
        sf_ref[...] = jnp.zeros_like(sf_ref)
        sb_ref[...] = jnp.zeros_like(sb_ref)

    row = lax.broadcasted_iota(jnp.int32, (c, c), 0)
    col = lax.broadcasted_iota(jnp.int32, (c, c), 1)
    tri_lower = jnp.where(col <= row, 1.0, 0.0).astype(BF16)
    tri_upper = jnp.where(col >= row, 1.0, 0.0).astype(BF16)
    xor = row ^ col
    heads = [slice(h * HEAD_DIM, (h + 1) * HEAD_DIM) for h in range(N_HEADS)]

    def chunk(ci, carry):
        rows = pl.ds(pl.multiple_of(ci * c, c), c)
        rows2 = pl.ds(pl.multiple_of((GLA_STEP_CHUNKS - 1 - ci) * c, c), c)

        gb2 = jnp.dot(tri_upper, lb2_ref[0, rows2, :], preferred_element_type=F32)
        g_first = gb2[0:1, :]
        qdec2 = (q2_ref[0, rows2, :].astype(F32) * jnp.exp2(gb2)).astype(BF16)
        kdec2 = (kb2_ref[0, rows2, :].astype(F32) * jnp.exp2(g_first - gb2)).astype(BF16)
        decay2 = jnp.exp2(g_first)
        for h, hs in enumerate(heads):
            s_b = sb_ref[h]
            o2_ref[0, rows2, hs] = _dot_nt(qdec2[:, hs], s_b.astype(BF16)).astype(o2_ref.dtype)
            sb_ref[h] = s_b * decay2[:, hs] + _dot_tn(v2_ref[0, rows2, hs], kdec2[:, hs])

        q16 = q_ref[0, rows, :]
        q, kf, kb = q16.astype(F32), kf_ref[0, rows, :].astype(F32), kb_ref[0, rows, :].astype(F32)
        lf16, lb16 = lf_ref[0, rows, :], lb_ref[0, rows, :]
        gf = jnp.dot(tri_lower, lf16, preferred_element_type=F32)
        gb = jnp.dot(tri_upper, lb16, preferred_element_type=F32)

        attn = [None] * N_HEADS
        for block in LEVELS:
            q_l, k_l = _level_operands(q, kf, kb, gf, gb, block)
            for h, hs in enumerate(heads):
                p = _dot_nt(q_l[:, hs], k_l[:, hs])
                attn[h] = p if attn[h] is None else jnp.where(xor < 2 * block, p, attn[h])
        odd = (lax.broadcasted_iota(jnp.int32, q.shape, 0) & 1) != 0
        step = jnp.where(odd, lf16.astype(F32), lb16.astype(F32))
        q_l = (q * jnp.exp2(step)).astype(BF16)
        k_l = jnp.where(odd, kb, kf).astype(BF16)
        k_d = (kf + kb).astype(BF16)
        for h, hs in enumerate(heads):
            attn[h] = jnp.where(xor < 2, _dot_nt(q_l[:, hs], k_l[:, hs]), attn[h])
            attn[h] = jnp.where(xor < 1, _dot_nt(q16[:, hs], k_d[:, hs]), attn[h])

        g_last = gf[c - 1:c, :]
        qdec = (q * jnp.exp2(gf)).astype(BF16)
        kdec = (kf * jnp.exp2(g_last - gf)).astype(BF16)
        decay = jnp.exp2(g_last)
        for h, hs in enumerate(heads):
            s_f = sf_ref[h]
            v = v_ref[0, rows, hs]
            o = jnp.dot(attn[h].astype(BF16), v, preferred_element_type=F32)
            o_ref[0, rows, hs] = (o + _dot_nt(qdec[:, hs], s_f.astype(BF16))).astype(o_ref.dtype)
            sf_ref[h] = s_f * decay[:, hs] + _dot_tn(v, kdec[:, hs])
        return carry

    lax.fori_loop(0, GLA_STEP_CHUNKS, chunk, 0, unroll=4)


def _gla(gates, k, l2f, val, batch, seq):
    rows = GLA_STEP_CHUNKS * REC_CHUNK
    n = seq // rows
    as3 = lambda a: a.reshape(batch, seq, a.shape[-1])

    def fwd(colblock):
        return pl.BlockSpec((1, rows, D_MODEL), lambda bi, ni: (bi, ni, colblock))

    def bwd(colblock):
        return pl.BlockSpec((1, rows, D_MODEL), lambda bi, ni: (bi, n - 1 - ni, colblock))

    out_shape = jax.ShapeDtypeStruct((batch, seq, D_MODEL), BF16)
    g3, k3, l3, v3 = as3(gates), as3(k), as3(l2f), as3(val)
    return pl.pallas_call(
        _gla_kernel,
        grid=(batch, n),
        in_specs=[fwd(0), fwd(0), fwd(1), fwd(0), fwd(1), fwd(0),
                  bwd(0), bwd(1), bwd(1), bwd(0)],
        out_specs=[fwd(0), bwd(0)],
        out_shape=[out_shape, out_shape],
        scratch_shapes=[pltpu.VMEM((N_HEADS, HEAD_DIM, HEAD_DIM), F32),
                        pltpu.VMEM((N_HEADS, HEAD_DIM, HEAD_DIM), F32)],
        compiler_params=_compiler_params(("parallel", "arbitrary")),
        name="gla",
    )(g3, k3, k3, l3, l3, v3,
      g3, k3, l3, v3)


def _merge_kernel(o1_ref, o2_ref, ga_ref, u_ref, v_ref, gb_ref, ma_ref, mb_ref, x_ref,
                  gnw_ref, ws_ref, bs_ref, wa_ref, wb_ref, wo_ref, nw_ref,
                  xo_ref, ho_ref):
    rows = x_ref.shape[0]

    o = o1_ref[...].astype(F32) + o2_ref[...].astype(F32)
    parts = []
    for h in range(N_HEADS):
        hs = slice(h * HEAD_DIM, (h + 1) * HEAD_DIM)
        oh = o[:, hs]
        oh = oh * lax.rsqrt(jnp.mean(oh * oh, axis=-1, keepdims=True) + RMS_EPS) * gnw_ref[...]
        parts.append((oh * ga_ref[:, hs].astype(F32)).astype(BF16))
    y_a = jnp.concatenate(parts, axis=1)

    row_parts = []
    for ci in range(rows // SG_CHUNK):
        rs = slice(ci * SG_CHUNK, (ci + 1) * SG_CHUNK)
        parts = []
        for g in range(SG_GROUPS):
            hs = slice(g * HEAD_DIM, (g + 1) * HEAD_DIM)
            mixed = jnp.dot(ws_ref[g], v_ref[rs, hs], preferred_element_type=F32) + bs_ref[g]
            parts.append((_gelu(u_ref[rs, hs].astype(F32)) * mixed * gb_ref[rs, hs].astype(F32)).astype(BF16))
        row_parts.append(jnp.concatenate(parts, axis=1))
    y_b = jnp.concatenate(row_parts, axis=0)

    merged = (ma_ref[...].astype(F32) * jnp.dot(y_a, wa_ref[...], preferred_element_type=F32)
              + mb_ref[...].astype(F32) * jnp.dot(y_b, wb_ref[...], preferred_element_type=F32))
    x_new = x_ref[...] + jnp.dot(merged.astype(BF16), wo_ref[...], preferred_element_type=F32)
    xo_ref[...] = x_new
    y = x_new * lax.rsqrt(jnp.mean(x_new * x_new, axis=-1, keepdims=True) + RMS_EPS)
    ho_ref[...] = (y * nw_ref[...]).astype(ho_ref.dtype)


def _merge(o1, o2, gates, plain, v, merges, x, gnorm_w, w_s, b_s, w_a, w_b, w_o, next_norm_w, h_dtype, block_rows=512):
    t = x.shape[0]

    def tok(colblock=0):
        return pl.BlockSpec((block_rows, D_MODEL), lambda i: (i, colblock))

    def whole(shape):
        return pl.BlockSpec(shape, lambda i: (0,) * len(shape))

    bs_full = jnp.broadcast_to(b_s.astype(F32)[:, :, None], (SG_GROUPS, SG_CHUNK, HEAD_DIM))
    return pl.pallas_call(
        _merge_kernel,
        grid=(t // block_rows,),
        in_specs=[tok(), tok(), tok(1), tok(1), tok(), tok(2), tok(0), tok(1), tok(),
                  whole((1, HEAD_DIM)),
                  whole((SG_GROUPS, SG_CHUNK, SG_CHUNK)), whole((SG_GROUPS, SG_CHUNK, HEAD_DIM)),
                  whole((D_MODEL, D_MODEL)), whole((D_MODEL, D_MODEL)), whole((D_MODEL, D_MODEL)),
                  whole((1, D_MODEL))],
        out_specs=[tok(), tok()],
        out_shape=[jax.ShapeDtypeStruct((t, D_MODEL), F32), jax.ShapeDtypeStruct((t, D_MODEL), h_dtype)],
        compiler_params=_compiler_params(("parallel",)),
        name="merge",
    )(o1, o2, gates, plain, v, gates, merges, merges, x,
      gnorm_w.reshape(1, HEAD_DIM), w_s.astype(BF16), bs_full, w_a, w_b, w_o, next_norm_w.reshape(1, D_MODEL))


def kernel(x, norm_w, w_in, lower_bounds, gnorm_w, ln_w, ln_b, w_s, b_s, w_proj_a, w_proj_b, w_out, final_norm_w):
    batch, seq, _ = x.shape
    t = batch * seq
    lb = _lower_bounds(lower_bounds)
    x2 = x.reshape(t, D_MODEL)
    h = _rmsnorm(x2, norm_w[0], BF16)
    for layer in range(DEPTH):
        last = layer == DEPTH - 1
        gates, k, l2f, plain, v, merges = _in_proj(h, w_in, layer, lb[layer], ln_w[layer], ln_b[layer])
        o1, o2 = _gla(gates, k, l2f, plain, batch, seq)
        x2, h = _merge(o1.reshape(t, D_MODEL), o2.reshape(t, D_MODEL), gates, plain, v, merges, x2,
                       gnorm_w[layer], w_s[layer], b_s[layer],
                       w_proj_a[layer].astype(BF16), w_proj_b[layer].astype(BF16), w_out[layer].astype(BF16),
                       final_norm_w if last else norm_w[layer + 1], F32 if last else BF16)
    return h.reshape(batch, seq, D_MODEL)
```

```python
import math

import jax
import jax.numpy as jnp
from jax import lax
from jax.experimental import pallas as pl
from jax.experimental.pallas import tpu as pltpu

D_MODEL = 1024
DEPTH = 4
HEAD_DIM = 128
N_HEADS = D_MODEL // HEAD_DIM
SG_CHUNK = 128
SG_GROUPS = D_MODEL // HEAD_DIM
RMS_EPS = 1e-6
LN_EPS = 1e-5
LB_FLOOR = 1e-20

COL_Q, COL_FF, COL_FB, COL_I, COL_GA, COL_U, COL_V, COL_GB, COL_MA, COL_MB = range(10)

REC_CHUNK = 128
LEVELS = (64, 32, 16, 8, 4, 2)
GLA_STEP_CHUNKS = 8

PROJ_BLOCK_ROWS = 4096
LOG2E = math.log2(math.e)
VMEM_LIMIT_BYTES = 56 * 1024 * 1024

F32 = jnp.float32
BF16 = jnp.bfloat16


def _compiler_params(semantics):
    return pltpu.CompilerParams(dimension_semantics=semantics, vmem_limit_bytes=VMEM_LIMIT_BYTES)


def _lower_bound_kernel(lb_ref, out_ref):
    x = lb_ref[...]
    m = jnp.max(x, axis=0, keepdims=True)
    e = jnp.exp(x - m)
    p = e / jnp.sum(e, axis=0, keepdims=True)
    acc = p[0:1]
    rows = [acc - p[0:1]]
    for layer in range(1, DEPTH):
        acc = acc + p[layer:layer + 1]
        rows.append(acc - p[0:1])
    out_ref[...] = jnp.concatenate(rows, axis=0)


def _lower_bounds(lower_bounds):
    flat = lower_bounds.astype(F32).reshape(DEPTH, 2 * D_MODEL)
    out = pl.pallas_call(
        _lower_bound_kernel,
        out_shape=jax.ShapeDtypeStruct((DEPTH, 2 * D_MODEL), F32),
        name="lower_bounds",
    )(flat)
    return out.reshape(DEPTH, 2, 1, D_MODEL)


def _rmsnorm_kernel(x_ref, w_ref, o_ref):
    x = x_ref[...]
    y = x * lax.rsqrt(jnp.mean(x * x, axis=-1, keepdims=True) + RMS_EPS)
    o_ref[...] = (y * w_ref[...]).astype(o_ref.dtype)


def _rmsnorm(x2d, w, out_dtype, block_rows=2048):
    t = x2d.shape[0]
    return pl.pallas_call(
        _rmsnorm_kernel,
        grid=(t // block_rows,),
        in_specs=[pl.BlockSpec((block_rows, D_MODEL), lambda i: (i, 0)),
                  pl.BlockSpec((1, D_MODEL), lambda i: (0, 0))],
        out_specs=pl.BlockSpec((block_rows, D_MODEL), lambda i: (i, 0)),
        out_shape=jax.ShapeDtypeStruct((t, D_MODEL), out_dtype),
        compiler_params=_compiler_params(("parallel",)),
        name="rmsnorm",
    )(x2d, w.reshape(1, D_MODEL))


def _sigmoid(x):
    return 1.0 / (1.0 + jnp.exp2(x * (-LOG2E)))


def _gelu(x):
    a = -2.0 * LOG2E * math.sqrt(2.0 / math.pi)
    t = (x * x) * (a * 0.044715) + a
    return x / (1.0 + jnp.exp2(x * t))


def _project_rows(h_ref, w_ref, wb_ref, emit, sub_rows):
    @pl.when(pl.program_id(1) == 0)
    def _():
        wb_ref[...] = w_ref[...].astype(wb_ref.dtype)

    blocks = [slice(r, r + sub_rows) for r in range(0, h_ref.shape[0], sub_rows)]
    acc = jnp.dot(h_ref[blocks[0], :], wb_ref[...], preferred_element_type=F32)
    for prev, rows in zip(blocks[:-1], blocks[1:]):
        nxt = jnp.dot(h_ref[rows, :], wb_ref[...], preferred_element_type=F32)
        emit(prev, acc)
        acc = nxt
    emit(blocks[-1], acc)


def _proj_plain_kernel(h_ref, w_ref, o_ref, wb_ref):
    def emit(rows, acc):
        o_ref[rows, :] = acc.astype(o_ref.dtype)
    _project_rows(h_ref, w_ref, wb_ref, emit, 2048)


def _proj_silu_kernel(h_ref, w_ref, scale_ref, o_ref, wb_ref):
    def emit(rows, acc):
        o_ref[rows, :] = (acc * _sigmoid(acc) * scale_ref[0]).astype(o_ref.dtype)
    _project_rows(h_ref, w_ref, wb_ref, emit, 256)


def _proj_sigmoid_kernel(h_ref, w_ref, o_ref, wb_ref):
    def emit(rows, acc):
        o_ref[rows, :] = _sigmoid(acc).astype(o_ref.dtype)
    _project_rows(h_ref, w_ref, wb_ref, emit, 128)


def _proj_gelu_ln_kernel(h_ref, w_ref, lnw_ref, lnb_ref, o_ref, wb_ref):
    def emit(rows, acc):
        v = _gelu(acc)
        vc = v - jnp.mean(v, axis=-1, keepdims=True)
        var = jnp.mean(vc * vc, axis=-1, keepdims=True)
        o_ref[rows, :] = (vc * lax.rsqrt(var + LN_EPS) * lnw_ref[...] + lnb_ref[...]).astype(o_ref.dtype)
    _project_rows(h_ref, w_ref, wb_ref, emit, 256)


def _proj_forget_kernel(h_ref, w_ref, lb_ref, k_ref, l2f_ref, wb_ref):
    lb = lb_ref[0]
    floor_lb, one_minus_lb = jnp.maximum(lb, LB_FLOOR), 1.0 - lb

    def emit(rows, acc):
        gated = one_minus_lb * _sigmoid(acc)
        k_ref[rows, :] = (one_minus_lb - gated).astype(k_ref.dtype)
        l2f_ref[rows, :] = jnp.log2(floor_lb + gated).astype(l2f_ref.dtype)
    _project_rows(h_ref, w_ref, wb_ref, emit, 128)


def _proj(kernel_fn, name, h, w, layer, col_of, n_cols, n_out=1, extra=()):
    t = h.shape[0]
    rows = PROJ_BLOCK_ROWS // n_out
    out_spec = pl.BlockSpec((rows, D_MODEL), lambda j, i: (i, j))
    out_shape = jax.ShapeDtypeStruct((t, n_cols * D_MODEL), BF16)
    res = pl.pallas_call(
        kernel_fn,
        grid=(n_cols, t // rows),
        in_specs=[pl.BlockSpec((rows, D_MODEL), lambda j, i: (i, 0)),
                  pl.BlockSpec((None, D_MODEL, D_MODEL), lambda j, i: (layer, 0, col_of(j)))]
                 + [pl.BlockSpec(shape, imap) for _, shape, imap in extra],
        out_specs=[out_spec] * n_out,
        out_shape=[out_shape] * n_out,
        scratch_shapes=[pltpu.VMEM((D_MODEL, D_MODEL), BF16)],
        compiler_params=_compiler_params(("parallel", "arbitrary")),
        name=name,
    )(h, w, *[a for a, _, _ in extra])
    return res if n_out > 1 else res[0]


def _in_proj(h, w, layer, lb_layer, ln_w, ln_b):
    row = lambda a: a.reshape(1, D_MODEL)
    const = lambda j, i: (0, 0)
    scales = jnp.ones((3, 1, D_MODEL), F32).at[0].multiply(HEAD_DIM ** -0.5)
    gates = _proj(_proj_silu_kernel, "proj_silu", h, w, layer, lambda j: (j * 7 + 1) // 2, 3,
                  extra=[(scales, (1, 1, D_MODEL), lambda j, i: (j, 0, 0))])
    k, l2f = _proj(_proj_forget_kernel, "proj_forget", h, w, layer, lambda j: j + COL_FF, 2, n_out=2,
                   extra=[(lb_layer, (1, 1, D_MODEL), lambda j, i: (j, 0, 0))])
    plain = _proj(_proj_plain_kernel, "proj_plain", h, w, layer, lambda j: COL_I + 2 * j, 2)
    v = _proj(_proj_gelu_ln_kernel, "proj_gelu_ln", h, w, layer, lambda j: j + COL_V, 1,
              extra=[(row(ln_w), (1, D_MODEL), const), (row(ln_b), (1, D_MODEL), const)])
    merges = _proj(_proj_sigmoid_kernel, "proj_sigmoid", h, w, layer, lambda j: j + COL_MA, 2)
    return gates, k, l2f, plain, v, merges


def _dot_nt(a, b):
    return lax.dot_general(a, b, (((1,), (1,)), ((), ())), preferred_element_type=F32)


def _dot_tn(a, b):
    return lax.dot_general(a, b, (((0,), (0,)), ((), ())), preferred_element_type=F32)


def _boundary_rows(g, block, row_in_parent):
    c, width = g.shape
    parent = 2 * block
    g3 = g.reshape(c // 8, 8, width)
    sub = lax.broadcasted_iota(jnp.int32, g3.shape, 1)
    r = jnp.broadcast_to(g3[:, row_in_parent:row_in_parent + 1, :], g3.shape)
    for p in range(1, 8 // parent):
        row = p * parent + row_in_parent
        r = jnp.where(sub >= p * parent, jnp.broadcast_to(g3[:, row:row + 1, :], g3.shape), r)
    return r.reshape(c, width)


def _level_operands(q, kf, kb, gf, gb, block):
    c = q.shape[0]
    if block >= 8:
        eq_parts, ek_parts, k_parts = [], [], []
        for blk in range(c // block):
            rows = slice(blk * block, (blk + 1) * block)
            edge = (blk | 1) * block
            if blk % 2:
                eq_parts.append(gf[rows] - gf[edge - 1:edge])
                ek_parts.append(gb[edge:edge + 1] - gb[rows])
                k_parts.append(kb[rows])
            else:
                eq_parts.append(gb[rows] - gb[edge:edge + 1])
                ek_parts.append(gf[edge - 1:edge] - gf[rows])
                k_parts.append(kf[rows])
        q_l = q * jnp.exp2(jnp.concatenate(eq_parts, axis=0))
        k_l = jnp.concatenate(k_parts, axis=0) * jnp.exp2(jnp.concatenate(ek_parts, axis=0))
    else:
        d_f = gf - _boundary_rows(gf, block, block - 1)
        d_b = gb - _boundary_rows(gb, block, block)
        odd = (lax.broadcasted_iota(jnp.int32, q.shape, 0) & block) != 0
        q_l = q * jnp.exp2(jnp.minimum(d_f, d_b))
        k_l = jnp.where(odd, kb, kf) * jnp.exp2(-jnp.maximum(d_f, d_b))
    return q_l.astype(BF16), k_l.astype(BF16)


def _gla_kernel(q_ref, kf_ref, kb_ref, lf_ref, lb_ref, v_ref,
                q2_ref, kb2_ref, lb2_ref, v2_ref,
                o_ref, o2_ref, sf_ref, sb_ref):
    c = REC_CHUNK

    @pl.when(pl.program_id(1) == 0)
    def _():
        sf_ref[...] = jnp.zeros_like(sf_ref)
        sb_ref[...] = jnp.zeros_like(sb_ref)

    row = lax.broadcasted_iota(jnp.int32, (c, c), 0)
    col = lax.broadcasted_iota(jnp.int32, (c, c), 1)
    tri_lower = jnp.where(col <= row, 1.0, 0.0).astype(BF16)
    tri_upper = jnp.where(col >= row, 1.0, 0.0).astype(BF16)
    xor = row ^ col
    heads = [slice(h * HEAD_DIM, (h + 1) * HEAD_DIM) for h in range(N_HEADS)]

    def chunk(ci, carry):
        rows = pl.ds(pl.multiple_of(ci * c, c), c)
        rows2 = pl.ds(pl.multiple_of((GLA_STEP_CHUNKS - 1 - ci) * c, c), c)

        gb2 = jnp.dot(tri_upper, lb2_ref[0, rows2, :], preferred_element_type=F32)
        g_first = gb2[0:1, :]
        qdec2 = (q2_ref[0, rows2, :].astype(F32) * jnp.exp2(gb2)).astype(BF16)
        kdec2 = (kb2_ref[0, rows2, :].astype(F32) * jnp.exp2(g_first - gb2)).astype(BF16)
        decay2 = jnp.exp2(g_first)
        for h, hs in enumerate(heads):
            s_b = sb_ref[h]
            o2_ref[0, rows2, hs] = _dot_nt(qdec2[:, hs], s_b.astype(BF16)).astype(o2_ref.dtype)
            sb_ref[h] = s_b * decay2[:, hs] + _dot_tn(v2_ref[0, rows2, hs], kdec2[:, hs])

        q16 = q_ref[0, rows, :]
        q, kf, kb = q16.astype(F32), kf_ref[0, rows, :].astype(F32), kb_ref[0, rows, :].astype(F32)
        lf16, lb16 = lf_ref[0, rows, :], lb_ref[0, rows, :]
        gf = jnp.dot(tri_lower, lf16, preferred_element_type=F32)
        gb = jnp.dot(tri_upper, lb16, preferred_element_type=F32)

        k_d = (kf + kb).astype(BF16)
        attn = [_dot_nt(q16[:, hs], k_d[:, hs]) for hs in heads]
        odd = (lax.broadcasted_iota(jnp.int32, q.shape, 0) & 1) != 0
        step = jnp.where(odd, lf16.astype(F32), lb16.astype(F32))
        q_l = (q * jnp.exp2(step)).astype(BF16)
        k_l = jnp.where(odd, kb, kf).astype(BF16)
        for h, hs in enumerate(heads):
            attn[h] = jnp.where(xor >= 1, _dot_nt(q_l[:, hs], k_l[:, hs]), attn[h])
        for block in reversed(LEVELS):
            q_l, k_l = _level_operands(q, kf, kb, gf, gb, block)
            for h, hs in enumerate(heads):
                attn[h] = jnp.where(xor >= block, _dot_nt(q_l[:, hs], k_l[:, hs]), attn[h])

        g_last = gf[c - 1:c, :]
        qdec = (q * jnp.exp2(gf)).astype(BF16)
        kdec = (kf * jnp.exp2(g_last - gf)).astype(BF16)
        decay = jnp.exp2(g_last)
        for h, hs in enumerate(heads):
            s_f = sf_ref[h]
            v = v_ref[0, rows, hs]
            o = jnp.dot(attn[h].astype(BF16), v, preferred_element_type=F32)
            o_ref[0, rows, hs] = (o + _dot_nt(qdec[:, hs], s_f.astype(BF16))).astype(o_ref.dtype)
            sf_ref[h] = s_f * decay[:, hs] + _dot_tn(v, kdec[:, hs])
        return carry

    lax.fori_loop(0, GLA_STEP_CHUNKS, chunk, 0, unroll=4)


def _gla(gates, k, l2f, val, batch, seq):
    rows = GLA_STEP_CHUNKS * REC_CHUNK
    n = seq // rows
    as3 = lambda a: a.reshape(batch, seq, a.shape[-1])

    def fwd(colblock):
        return pl.BlockSpec((1, rows, D_MODEL), lambda bi, ni: (bi, ni, colblock))

    def bwd(colblock):
        return pl.BlockSpec((1, rows, D_MODEL), lambda bi, ni: (bi, n - 1 - ni, colblock))

    out_shape = jax.ShapeDtypeStruct((batch, seq, D_MODEL), BF16)
    g3, k3, l3, v3 = as3(gates), as3(k), as3(l2f), as3(val)
    return pl.pallas_call(
        _gla_kernel,
        grid=(batch, n),
        in_specs=[fwd(0), fwd(0), fwd(1), fwd(0), fwd(1), fwd(0),
                  bwd(0), bwd(1), bwd(1), bwd(0)],
        out_specs=[fwd(0), bwd(0)],
        out_shape=[out_shape, out_shape],
        scratch_shapes=[pltpu.VMEM((N_HEADS, HEAD_DIM, HEAD_DIM), F32),
                        pltpu.VMEM((N_HEADS, HEAD_DIM, HEAD_DIM), F32)],
        compiler_params=_compiler_params(("parallel", "arbitrary")),
        name="gla",
    )(g3, k3, k3, l3, l3, v3,
      g3, k3, l3, v3)


def _merge_kernel(o1_ref, o2_ref, ga_ref, u_ref, v_ref, gb_ref, ma_ref, mb_ref, x_ref,
                  gnw_ref, ws_ref, bs_ref, wa_ref, wb_ref, wo_ref, nw_ref,
                  xo_ref, ho_ref):
    rows = x_ref.shape[0]

    o = o1_ref[...].astype(F32) + o2_ref[...].astype(F32)
    parts = []
    for h in range(N_HEADS):
        hs = slice(h * HEAD_DIM, (h + 1) * HEAD_DIM)
        oh = o[:, hs]
        oh = oh * lax.rsqrt(jnp.mean(oh * oh, axis=-1, keepdims=True) + RMS_EPS) * gnw_ref[...]
        parts.append((oh * ga_ref[:, hs].astype(F32)).astype(BF16))
    y_a = jnp.concatenate(parts, axis=1)

    row_parts = []
    for ci in range(rows // SG_CHUNK):
        rs = slice(ci * SG_CHUNK, (ci + 1) * SG_CHUNK)
        parts = []
        for g in range(SG_GROUPS):
            hs = slice(g * HEAD_DIM, (g + 1) * HEAD_DIM)
            mixed = jnp.dot(ws_ref[g], v_ref[rs, hs], preferred_element_type=F32) + bs_ref[g]
            parts.append((_gelu(u_ref[rs, hs].astype(F32)) * mixed * gb_ref[rs, hs].astype(F32)).astype(BF16))
        row_parts.append(jnp.concatenate(parts, axis=1))
    y_b = jnp.concatenate(row_parts, axis=0)

    merged = (ma_ref[...].astype(F32) * jnp.dot(y_a, wa_ref[...], preferred_element_type=F32)
              + mb_ref[...].astype(F32) * jnp.dot(y_b, wb_ref[...], preferred_element_type=F32))
    x_new = x_ref[...] + jnp.dot(merged.astype(BF16), wo_ref[...], preferred_element_type=F32)
    xo_ref[...] = x_new
    y = x_new * lax.rsqrt(jnp.mean(x_new * x_new, axis=-1, keepdims=True) + RMS_EPS)
    ho_ref[...] = (y * nw_ref[...]).astype(ho_ref.dtype)


def _merge(o1, o2, gates, plain, v, merges, x, gnorm_w, w_s, b_s, w_a, w_b, w_o, next_norm_w, h_dtype, block_rows=512):
    t = x.shape[0]

    def tok(colblock=0):
        return pl.BlockSpec((block_rows, D_MODEL), lambda i: (i, colblock))

    def whole(shape):
        return pl.BlockSpec(shape, lambda i: (0,) * len(shape))

    bs_full = jnp.broadcast_to(b_s.astype(F32)[:, :, None], (SG_GROUPS, SG_CHUNK, HEAD_DIM))
    return pl.pallas_call(
        _merge_kernel,
        grid=(t // block_rows,),
        in_specs=[tok(), tok(), tok(1), tok(1), tok(), tok(2), tok(0), tok(1), tok(),
                  whole((1, HEAD_DIM)),
                  whole((SG_GROUPS, SG_CHUNK, SG_CHUNK)), whole((SG_GROUPS, SG_CHUNK, HEAD_DIM)),
                  whole((D_MODEL, D_MODEL)), whole((D_MODEL, D_MODEL)), whole((D_MODEL, D_MODEL)),
                  whole((1, D_MODEL))],
        out_specs=[tok(), tok()],
        out_shape=[jax.ShapeDtypeStruct((t, D_MODEL), F32), jax.ShapeDtypeStruct((t, D_MODEL), h_dtype)],
        compiler_params=_compiler_params(("parallel",)),
        name="merge",
    )(o1, o2, gates, plain, v, gates, merges, merges, x,
      gnorm_w.reshape(1, HEAD_DIM), w_s.astype(BF16), bs_full, w_a, w_b, w_o, next_norm_w.reshape(1, D_MODEL))


def kernel(x, norm_w, w_in, lower_bounds, gnorm_w, ln_w, ln_b, w_s, b_s, w_proj_a, w_proj_b, w_out, final_norm_w):
    batch, seq, _ = x.shape
    t = batch * seq
    lb = _lower_bounds(lower_bounds)
    x2 = x.reshape(t, D_MODEL)
    h = _rmsnorm(x2, norm_w[0], BF16)
    for layer in range(DEPTH):
        last = layer == DEPTH - 1
        gates, k, l2f, plain, v, merges = _in_proj(h, w_in, layer, lb[layer], ln_w[layer], ln_b[layer])
        o1, o2 = _gla(gates, k, l2f, plain, batch, seq)
        x2, h = _merge(o1.reshape(t, D_MODEL), o2.reshape(t, D_MODEL), gates, plain, v, merges, x2,
                       gnorm_w[layer], w_s[layer], b_s[layer],
                       w_proj_a[layer].astype(BF16), w_proj_b[layer].astype(BF16), w_out[layer].astype(BF16),
                       final_norm_w if last else norm_w[layer + 1], F32 if last else BF16)
    return h.reshape(batch, seq, D_MODEL)
```

```python
import math

import jax
import jax.numpy as jnp
from jax import lax
from jax.experimental import pallas as pl
from jax.experimental.pallas import tpu as pltpu

D_MODEL = 1024
DEPTH = 4
HEAD_DIM = 128
N_HEADS = D_MODEL // HEAD_DIM
SG_CHUNK = 128
SG_GROUPS = D_MODEL // HEAD_DIM
RMS_EPS = 1e-6
LN_EPS = 1e-5
LB_FLOOR = 1e-20

COL_Q, COL_FF, COL_FB, COL_I, COL_GA, COL_U, COL_V, COL_GB, COL_MA, COL_MB = range(10)

REC_CHUNK = 128
LEVELS = (64, 32, 16, 8, 4, 2)
GLA_STEP_CHUNKS = 8

PROJ_BLOCK_ROWS = 4096
LOG2E = math.log2(math.e)
VMEM_LIMIT_BYTES = 56 * 1024 * 1024

F32 = jnp.float32
BF16 = jnp.bfloat16


def _compiler_params(semantics):
    return pltpu.CompilerParams(dimension_semantics=semantics, vmem_limit_bytes=VMEM_LIMIT_BYTES)


def _lower_bound_kernel(lb_ref, out_ref):
    x = lb_ref[...]
    m = jnp.max(x, axis=0, keepdims=True)
    e = jnp.exp(x - m)
    p = e / jnp.sum(e, axis=0, keepdims=True)
    acc = p[0:1]
    rows = [acc - p[0:1]]
    for layer in range(1, DEPTH):
        acc = acc + p[layer:layer + 1]
        rows.append(acc - p[0:1])
    out_ref[...] = jnp.concatenate(rows, axis=0)


def _lower_bounds(lower_bounds):
    flat = lower_bounds.astype(F32).reshape(DEPTH, 2 * D_MODEL)
    out = pl.pallas_call(
        _lower_bound_kernel,
        out_shape=jax.ShapeDtypeStruct((DEPTH, 2 * D_MODEL), F32),
        name="lower_bounds",
    )(flat)
    return out.reshape(DEPTH, 2, 1, D_MODEL)


def _rmsnorm_kernel(x_ref, w_ref, o_ref):
    x = x_ref[...]
    y = x * lax.rsqrt(jnp.mean(x * x, axis=-1, keepdims=True) + RMS_EPS)
    o_ref[...] = (y * w_ref[...]).astype(o_ref.dtype)


def _rmsnorm(x2d, w, out_dtype, block_rows=2048):
    t = x2d.shape[0]
    return pl.pallas_call(
        _rmsnorm_kernel,
        grid=(t // block_rows,),
        in_specs=[pl.BlockSpec((block_rows, D_MODEL), lambda i: (i, 0)),
                  pl.BlockSpec((1, D_MODEL), lambda i: (0, 0))],
        out_specs=pl.BlockSpec((block_rows, D_MODEL), lambda i: (i, 0)),
        out_shape=jax.ShapeDtypeStruct((t, D_MODEL), out_dtype),
        compiler_params=_compiler_params(("parallel",)),
        name="rmsnorm",
    )(x2d, w.reshape(1, D_MODEL))


def _sigmoid(x):
    return 1.0 / (1.0 + jnp.exp2(x * (-LOG2E)))


def _gelu(x):
    a = -2.0 * LOG2E * math.sqrt(2.0 / math.pi)
    t = (x * x) * (a * 0.044715) + a
    return x / (1.0 + jnp.exp2(x * t))


def _project_rows(h_ref, w_ref, wb_ref, emit, sub_rows):
    @pl.when(pl.program_id(1) == 0)
    def _():
        wb_ref[...] = w_ref[...].astype(wb_ref.dtype)

    blocks = [slice(r, r + sub_rows) for r in range(0, h_ref.shape[0], sub_rows)]
    acc = jnp.dot(h_ref[blocks[0], :], wb_ref[...], preferred_element_type=F32)
    for prev, rows in zip(blocks[:-1], blocks[1:]):
        nxt = jnp.dot(h_ref[rows, :], wb_ref[...], preferred_element_type=F32)
        emit(prev, acc)
        acc = nxt
    emit(blocks[-1], acc)


def _proj_plain_kernel(h_ref, w_ref, o_ref, wb_ref):
    def emit(rows, acc):
        o_ref[rows, :] = acc.astype(o_ref.dtype)
    _project_rows(h_ref, w_ref, wb_ref, emit, 2048)


def _proj_silu_kernel(h_ref, w_ref, scale_ref, o_ref, wb_ref):
    def emit(rows, acc):
        o_ref[rows, :] = (acc * _sigmoid(acc) * scale_ref[0]).astype(o_ref.dtype)
    _project_rows(h_ref, w_ref, wb_ref, emit, 256)


def _proj_sigmoid_kernel(h_ref, w_ref, o_ref, wb_ref):
    def emit(rows, acc):
        o_ref[rows, :] = _sigmoid(acc).astype(o_ref.dtype)
    _project_rows(h_ref, w_ref, wb_ref, emit, 128)


def _proj_gelu_ln_kernel(h_ref, w_ref, lnw_ref, lnb_ref, o_ref, wb_ref):
    def emit(rows, acc):
        v = _gelu(acc)
        vc = v - jnp.mean(v, axis=-1, keepdims=True)
        var = jnp.mean(vc * vc, axis=-1, keepdims=True)
        o_ref[rows, :] = (vc * lax.rsqrt(var + LN_EPS) * lnw_ref[...] + lnb_ref[...]).astype(o_ref.dtype)
    _project_rows(h_ref, w_ref, wb_ref, emit, 256)


def _proj_forget_kernel(h_ref, w_ref, lb_ref, k_ref, l2f_ref, wb_ref):
    lb = lb_ref[0]
    floor_lb, one_minus_lb = jnp.maximum(lb, LB_FLOOR), 1.0 - lb

    def emit(rows, acc):
        gated = one_minus_lb * _sigmoid(acc)
        k_ref[rows, :] = (one_minus_lb - gated).astype(k_ref.dtype)
        l2f_ref[rows, :] = jnp.log2(floor_lb + gated).astype(l2f_ref.dtype)
    _project_rows(h_ref, w_ref, wb_ref, emit, 128)


def _proj(kernel_fn, name, h, w, layer, col_of, n_cols, n_out=1, extra=()):
    t = h.shape[0]
    rows = PROJ_BLOCK_ROWS // n_out
    out_spec = pl.BlockSpec((rows, D_MODEL), lambda j, i: (i, j))
    out_shape = jax.ShapeDtypeStruct((t, n_cols * D_MODEL), BF16)
    res = pl.pallas_call(
        kernel_fn,
        grid=(n_cols, t // rows),
        in_specs=[pl.BlockSpec((rows, D_MODEL), lambda j, i: (i, 0)),
                  pl.BlockSpec((None, D_MODEL, D_MODEL), lambda j, i: (layer, 0, col_of(j)))]
                 + [pl.BlockSpec(shape, imap) for _, shape, imap in extra],
        out_specs=[out_spec] * n_out,
        out_shape=[out_shape] * n_out,
        scratch_shapes=[pltpu.VMEM((D_MODEL, D_MODEL), BF16)],
        compiler_params=_compiler_params(("parallel", "arbitrary")),
        name=name,
    )(h, w, *[a for a, _, _ in extra])
    return res if n_out > 1 else res[0]


def _in_proj(h, w, layer, lb_layer, ln_w, ln_b):
    row = lambda a: a.reshape(1, D_MODEL)
    const = lambda j, i: (0, 0)
    scales = jnp.ones((3, 1, D_MODEL), F32).at[0].multiply(HEAD_DIM ** -0.5)
    gates = _proj(_proj_silu_kernel, "proj_silu", h, w, layer, lambda j: (j * 7 + 1) // 2, 3,
                  extra=[(scales, (1, 1, D_MODEL), lambda j, i: (j, 0, 0))])
    k, l2f = _proj(_proj_forget_kernel, "proj_forget", h, w, layer, lambda j: j + COL_FF, 2, n_out=2,
                   extra=[(lb_layer, (1, 1, D_MODEL), lambda j, i: (j, 0, 0))])
    plain = _proj(_proj_plain_kernel, "proj_plain", h, w, layer, lambda j: COL_I + 2 * j, 2)
    v = _proj(_proj_gelu_ln_kernel, "proj_gelu_ln", h, w, layer, lambda j: j + COL_V, 1,
              extra=[(row(ln_w), (1, D_MODEL), const), (row(ln_b), (1, D_MODEL), const)])
    merges = _proj(_proj_sigmoid_kernel, "proj_sigmoid", h, w, layer, lambda j: j + COL_MA, 2)
    return gates, k, l2f, plain, v, merges


def _dot_nt(a, b):
    return lax.dot_general(a, b, (((1,), (1,)), ((), ())), preferred_element_type=F32)


def _dot_tn(a, b):
    return lax.dot_general(a, b, (((0,), (0,)), ((), ())), preferred_element_type=F32)


def _boundary_rows(g, block, row_in_parent):
    c, width = g.shape
    parent = 2 * block
    g3 = g.reshape(c // 8, 8, width)
    sub = lax.broadcasted_iota(jnp.int32, g3.shape, 1)
    r = jnp.broadcast_to(g3[:, row_in_parent:row_in_parent + 1, :], g3.shape)
    for p in range(1, 8 // parent):
        row = p * parent + row_in_parent
        r = jnp.where(sub >= p * parent, jnp.broadcast_to(g3[:, row:row + 1, :], g3.shape), r)
    return r.reshape(c, width)


def _level_operands(q, kf, kb, gf, gb, block):
    c = q.shape[0]
    if block >= 8:
        eq_parts, ek_parts, k_parts = [], [], []
        for blk in range(c // block):
            rows = slice(blk * block, (blk + 1) * block)
            edge = (blk | 1) * block
            if blk % 2:
                eq_parts.append(gf[rows] - gf[edge - 1:edge])
                ek_parts.append(gb[edge:edge + 1] - gb[rows])
                k_parts.append(kb[rows])
            else:
                eq_parts.append(gb[rows] - gb[edge:edge + 1])
                ek_parts.append(gf[edge - 1:edge] - gf[rows])
                k_parts.append(kf[rows])
        q_l = q * jnp.exp2(jnp.concatenate(eq_parts, axis=0))
        k_l = jnp.concatenate(k_parts, axis=0) * jnp.exp2(jnp.concatenate(ek_parts, axis=0))
    else:
        d_f = gf - _boundary_rows(gf, block, block - 1)
        d_b = gb - _boundary_rows(gb, block, block)
        odd = (lax.broadcasted_iota(jnp.int32, q.shape, 0) & block) != 0
        q_l = q * jnp.exp2(jnp.minimum(d_f, d_b))
        k_l = jnp.where(odd, kb, kf) * jnp.exp2(-jnp.maximum(d_f, d_b))
    return q_l.astype(BF16), k_l.astype(BF16)


def _gla_kernel(q_ref, kf_ref, kb_ref, lf_ref, lb_ref, v_ref,
                q2_ref, kb2_ref, lb2_ref, v2_ref,
                o_ref, o2_ref, sf_ref, sb_ref):
    c = REC_CHUNK

    @pl.when(pl.program_id(1) == 0)
    def _():
        sf_ref[...] = jnp.zeros_like(sf_ref)
        sb_ref[...] = jnp.zeros_like(sb_ref)

    row = lax.broadcasted_iota(jnp.int32, (c, c), 0)
    col = lax.broadcasted_iota(jnp.int32, (c, c), 1)
    tri_lower = jnp.where(col <= row, 1.0, 0.0).astype(BF16)
    tri_upper = jnp.where(col >= row, 1.0, 0.0).astype(BF16)
    xor = row ^ col
    heads = [slice(h * HEAD_DIM, (h + 1) * HEAD_DIM) for h in range(N_HEADS)]

    def chunk(ci, carry):
        rows = pl.ds(pl.multiple_of(ci * c, c), c)
        rows2 = pl.ds(pl.multiple_of((GLA_STEP_CHUNKS - 1 - ci) * c, c), c)

        gb2 = jnp.dot(tri_upper, lb2_ref[0, rows2, :], preferred_element_type=F32)
        g_first = gb2[0:1, :]
        qdec2 = (q2_ref[0, rows2, :].astype(F32) * jnp.exp2(gb2)).astype(BF16)
        kdec2 = (kb2_ref[0, rows2, :].astype(F32) * jnp.exp2(g_first - gb2)).astype(BF16)
        decay2 = jnp.exp2(g_first)
        for h, hs in enumerate(heads):
            s_b = sb_ref[h]
            o2_ref[0, rows2, hs] = _dot_nt(qdec2[:, hs], s_b.astype(BF16)).astype(o2_ref.dtype)
            sb_ref[h] = s_b * decay2[:, hs] + _dot_tn(v2_ref[0, rows2, hs], kdec2[:, hs])

        q16 = q_ref[0, rows, :]
        q, kf, kb = q16.astype(F32), kf_ref[0, rows, :].astype(F32), kb_ref[0, rows, :].astype(F32)
        lf16, lb16 = lf_ref[0, rows, :], lb_ref[0, rows, :]
        gf = jnp.dot(tri_lower, lf16, preferred_element_type=F32)
        gb = jnp.dot(tri_upper, lb16, preferred_element_type=F32)

        k_d = (kf + kb).astype(BF16)
        prods = [[_dot_nt(q16[:, hs], k_d[:, hs])] for hs in heads]
        odd = (lax.broadcasted_iota(jnp.int32, q.shape, 0) & 1) != 0
        step = jnp.where(odd, lf16.astype(F32), lb16.astype(F32))
        q_l = (q * jnp.exp2(step)).astype(BF16)
        k_l = jnp.where(odd, kb, kf).astype(BF16)
        for h, hs in enumerate(heads):
            prods[h].append(_dot_nt(q_l[:, hs], k_l[:, hs]))
        for block in reversed(LEVELS):
            q_l, k_l = _level_operands(q, kf, kb, gf, gb, block)
            for h, hs in enumerate(heads):
                prods[h].append(_dot_nt(q_l[:, hs], k_l[:, hs]))
        bits = [0, 1] + list(reversed(LEVELS))
        attn = []
        for h in range(N_HEADS):
            nodes = list(zip(bits, prods[h]))
            while len(nodes) > 1:
                nodes = [(lo_bit, jnp.where(xor >= hi_bit, hi, lo))
                         for (lo_bit, lo), (hi_bit, hi) in zip(nodes[0::2], nodes[1::2])]
            attn.append(nodes[0][1])

        g_last = gf[c - 1:c, :]
        qdec = (q * jnp.exp2(gf)).astype(BF16)
        kdec = (kf * jnp.exp2(g_last - gf)).astype(BF16)
        decay = jnp.exp2(g_last)
        for h, hs in enumerate(heads):
            s_f = sf_ref[h]
            v = v_ref[0, rows, hs]
            o = jnp.dot(attn[h].astype(BF16), v, preferred_element_type=F32)
            o_ref[0, rows, hs] = (o + _dot_nt(qdec[:, hs], s_f.astype(BF16))).astype(o_ref.dtype)
            sf_ref[h] = s_f * decay[:, hs] + _dot_tn(v, kdec[:, hs])
        return carry

    lax.fori_loop(0, GLA_STEP_CHUNKS, chunk, 0, unroll=4)


def _gla(gates, k, l2f, val, batch, seq):
    rows = GLA_STEP_CHUNKS * REC_CHUNK
    n = seq // rows
    as3 = lambda a: a.reshape(batch, seq, a.shape[-1])

    def fwd(colblock):
        return pl.BlockSpec((1, rows, D_MODEL), lambda bi, ni: (bi, ni, colblock))

    def bwd(colblock):
        return pl.BlockSpec((1, rows, D_MODEL), lambda bi, ni: (bi, n - 1 - ni, colblock))

    out_shape = jax.ShapeDtypeStruct((batch, seq, D_MODEL), BF16)
    g3, k3, l3, v3 = as3(gates), as3(k), as3(l2f), as3(val)
    return pl.pallas_call(
        _gla_kernel,
        grid=(batch, n),
        in_specs=[fwd(0), fwd(0), fwd(1), fwd(0), fwd(1), fwd(0),
                  bwd(0), bwd(1), bwd(1), bwd(0)],
        out_specs=[fwd(0), bwd(0)],
        out_shape=[out_shape, out_shape],
        scratch_shapes=[pltpu.VMEM((N_HEADS, HEAD_DIM, HEAD_DIM), F32),
                        pltpu.VMEM((N_HEADS, HEAD_DIM, HEAD_DIM), F32)],
        compiler_params=_compiler_params(("parallel", "arbitrary")),
        name="gla",
    )(g3, k3, k3, l3, l3, v3,
      g3, k3, l3, v3)


def _merge_kernel(o1_ref, o2_ref, ga_ref, u_ref, v_ref, gb_ref, ma_ref, mb_ref, x_ref,
                  gnw_ref, ws_ref, bs_ref, wa_ref, wb_ref, wo_ref, nw_ref,
                  xo_ref, ho_ref):
    rows = x_ref.shape[0]

    parts = []
    for h in range(N_HEADS):
        hs = slice(h * HEAD_DIM, (h + 1) * HEAD_DIM)
        oh = o1_ref[:, hs].astype(F32) + o2_ref[:, hs].astype(F32)
        oh = oh * lax.rsqrt(jnp.mean(oh * oh, axis=-1, keepdims=True) + RMS_EPS) * gnw_ref[...]
        parts.append((oh * ga_ref[:, hs].astype(F32)).astype(BF16))
    y_a = jnp.concatenate(parts, axis=1)

    row_parts = []
    for ci in range(rows // SG_CHUNK):
        rs = slice(ci * SG_CHUNK, (ci + 1) * SG_CHUNK)
        parts = []
        for g in range(SG_GROUPS):
            hs = slice(g * HEAD_DIM, (g + 1) * HEAD_DIM)
            mixed = jnp.dot(ws_ref[g], v_ref[rs, hs], preferred_element_type=F32) + bs_ref[g]
            parts.append((_gelu(u_ref[rs, hs].astype(F32)) * mixed * gb_ref[rs, hs].astype(F32)).astype(BF16))
        row_parts.append(jnp.concatenate(parts, axis=1))
    y_b = jnp.concatenate(row_parts, axis=0)

    merged = (ma_ref[...].astype(F32) * jnp.dot(y_a, wa_ref[...], preferred_element_type=F32)
              + mb_ref[...].astype(F32) * jnp.dot(y_b, wb_ref[...], preferred_element_type=F32))
    x_new = x_ref[...] + jnp.dot(merged.astype(BF16), wo_ref[...], preferred_element_type=F32)
    xo_ref[...] = x_new
    y = x_new * lax.rsqrt(jnp.mean(x_new * x_new, axis=-1, keepdims=True) + RMS_EPS)
    ho_ref[...] = (y * nw_ref[...]).astype(ho_ref.dtype)


def _merge(o1, o2, gates, plain, v, merges, x, gnorm_w, w_s, b_s, w_a, w_b, w_o, next_norm_w, h_dtype, block_rows=512):
    t = x.shape[0]

    def tok(colblock=0):
        return pl.BlockSpec((block_rows, D_MODEL), lambda i: (i, colblock))

    def whole(shape):
        return pl.BlockSpec(shape, lambda i: (0,) * len(shape))

    bs_full = jnp.broadcast_to(b_s.astype(F32)[:, :, None], (SG_GROUPS, SG_CHUNK, HEAD_DIM))
    return pl.pallas_call(
        _merge_kernel,
        grid=(t // block_rows,),
        in_specs=[tok(), tok(), tok(1), tok(1), tok(), tok(2), tok(0), tok(1), tok(),
                  whole((1, HEAD_DIM)),
                  whole((SG_GROUPS, SG_CHUNK, SG_CHUNK)), whole((SG_GROUPS, SG_CHUNK, HEAD_DIM)),
                  whole((D_MODEL, D_MODEL)), whole((D_MODEL, D_MODEL)), whole((D_MODEL, D_MODEL)),
                  whole((1, D_MODEL))],
        out_specs=[tok(), tok()],
        out_shape=[jax.ShapeDtypeStruct((t, D_MODEL), F32), jax.ShapeDtypeStruct((t, D_MODEL), h_dtype)],
        compiler_params=_compiler_params(("parallel",)),
        name="merge",
    )(o1, o2, gates, plain, v, gates, merges, merges, x,
      gnorm_w.reshape(1, HEAD_DIM), w_s.astype(BF16), bs_full, w_a, w_b, w_o, next_norm_w.reshape(1, D_MODEL))


def kernel(x, norm_w, w_in, lower_bounds, gnorm_w, ln_w, ln_b, w_s, b_s, w_proj_a, w_proj_b, w_out, final_norm_w):
    batch, seq, _ = x.shape
    t = batch * seq
    lb = _lower_bounds(lower_bounds)
    x2 = x.reshape(t, D_MODEL)
    h = _rmsnorm(x2, norm_w[0], BF16)
    for layer in range(DEPTH):
        last = layer == DEPTH - 1
        gates, k, l2f, plain, v, merges = _in_proj(h, w_in, layer, lb[layer], ln_w[layer], ln_b[layer])
        o1, o2 = _gla(gates, k, l2f, plain, batch, seq)
        x2, h = _merge(o1.reshape(t, D_MODEL), o2.reshape(t, D_MODEL), gates, plain, v, merges, x2,
                       gnorm_w[layer], w_s[layer], b_s[layer],
                       w_proj_a[layer].astype(BF16), w_proj_b[layer].astype(BF16), w_out[layer].astype(BF16),
                       final_norm_w if last else norm_w[layer + 1], F32 if last else BF16)
    return h.reshape(batch, seq, D_MODEL)
```

```python
import math

import jax
import jax.numpy as jnp
from jax import lax
from jax.experimental import pallas as pl
from jax.experimental.pallas import tpu as pltpu

D_MODEL = 1024
DEPTH = 4
HEAD_DIM = 128
N_HEADS = D_MODEL // HEAD_DIM
SG_CHUNK = 128
SG_GROUPS = D_MODEL // HEAD_DIM
RMS_EPS = 1e-6
LN_EPS = 1e-5
LB_FLOOR = 1e-20

COL_Q, COL_FF, COL_FB, COL_I, COL_GA, COL_U, COL_V, COL_GB, COL_MA, COL_MB = range(10)

REC_CHUNK = 128
LEVELS = (64, 32, 16, 8, 4, 2)
GLA_STEP_CHUNKS = 8

PROJ_BLOCK_ROWS = 4096
LOG2E = math.log2(math.e)
VMEM_LIMIT_BYTES = 56 * 1024 * 1024

F32 = jnp.float32
BF16 = jnp.bfloat16


def _compiler_params(semantics):
    return pltpu.CompilerParams(dimension_semantics=semantics, vmem_limit_bytes=VMEM_LIMIT_BYTES)


def _lower_bound_kernel(lb_ref, out_ref):
    x = lb_ref[...]
    m = jnp.max(x, axis=0, keepdims=True)
    e = jnp.exp(x - m)
    p = e / jnp.sum(e, axis=0, keepdims=True)
    acc = p[0:1]
    rows = [acc - p[0:1]]
    for layer in range(1, DEPTH):
        acc = acc + p[layer:layer + 1]
        rows.append(acc - p[0:1])
    out_ref[...] = jnp.concatenate(rows, axis=0)


def _lower_bounds(lower_bounds):
    flat = lower_bounds.astype(F32).reshape(DEPTH, 2 * D_MODEL)
    out = pl.pallas_call(
        _lower_bound_kernel,
        out_shape=jax.ShapeDtypeStruct((DEPTH, 2 * D_MODEL), F32),
        name="lower_bounds",
    )(flat)
    return out.reshape(DEPTH, 2, 1, D_MODEL)


def _rmsnorm_kernel(x_ref, w_ref, o_ref):
    x = x_ref[...]
    y = x * lax.rsqrt(jnp.mean(x * x, axis=-1, keepdims=True) + RMS_EPS)
    o_ref[...] = (y * w_ref[...]).astype(o_ref.dtype)


def _rmsnorm(x2d, w, out_dtype, block_rows=2048):
    t = x2d.shape[0]
    return pl.pallas_call(
        _rmsnorm_kernel,
        grid=(t // block_rows,),
        in_specs=[pl.BlockSpec((block_rows, D_MODEL), lambda i: (i, 0)),
                  pl.BlockSpec((1, D_MODEL), lambda i: (0, 0))],
        out_specs=pl.BlockSpec((block_rows, D_MODEL), lambda i: (i, 0)),
        out_shape=jax.ShapeDtypeStruct((t, D_MODEL), out_dtype),
        compiler_params=_compiler_params(("parallel",)),
        name="rmsnorm",
    )(x2d, w.reshape(1, D_MODEL))


def _sigmoid(x):
    return 1.0 / (1.0 + jnp.exp2(x * (-LOG2E)))


def _gelu(x):
    a = -2.0 * LOG2E * math.sqrt(2.0 / math.pi)
    t = (x * x) * (a * 0.044715) + a
    return x / (1.0 + jnp.exp2(x * t))


def _project_rows(h_ref, w_ref, wb_ref, emit, sub_rows):
    @pl.when(pl.program_id(1) == 0)
    def _():
        wb_ref[...] = w_ref[...].astype(wb_ref.dtype)

    blocks = [slice(r, r + sub_rows) for r in range(0, h_ref.shape[0], sub_rows)]
    acc = jnp.dot(h_ref[blocks[0], :], wb_ref[...], preferred_element_type=F32)
    for prev, rows in zip(blocks[:-1], blocks[1:]):
        nxt = jnp.dot(h_ref[rows, :], wb_ref[...], preferred_element_type=F32)
        emit(prev, acc)
        acc = nxt
    emit(blocks[-1], acc)


def _proj_plain_kernel(h_ref, w_ref, o_ref, wb_ref):
    def emit(rows, acc):
        o_ref[rows, :] = acc.astype(o_ref.dtype)
    _project_rows(h_ref, w_ref, wb_ref, emit, 2048)


def _proj_silu_kernel(h_ref, w_ref, scale_ref, o_ref, wb_ref):
    def emit(rows, acc):
        o_ref[rows, :] = (acc * _sigmoid(acc) * scale_ref[0]).astype(o_ref.dtype)
    _project_rows(h_ref, w_ref, wb_ref, emit, 256)


def _proj_sigmoid_kernel(h_ref, w_ref, o_ref, wb_ref):
    def emit(rows, acc):
        o_ref[rows, :] = _sigmoid(acc).astype(o_ref.dtype)
    _project_rows(h_ref, w_ref, wb_ref, emit, 128)


def _proj_gelu_ln_kernel(h_ref, w_ref, lnw_ref, lnb_ref, o_ref, wb_ref):
    def emit(rows, acc):
        v = _gelu(acc)
        vc = v - jnp.mean(v, axis=-1, keepdims=True)
        var = jnp.mean(vc * vc, axis=-1, keepdims=True)
        o_ref[rows, :] = (vc * lax.rsqrt(var + LN_EPS) * lnw_ref[...] + lnb_ref[...]).astype(o_ref.dtype)
    _project_rows(h_ref, w_ref, wb_ref, emit, 256)


def _proj_forget_kernel(h_ref, w_ref, lb_ref, k_ref, l2f_ref, wb_ref):
    lb = lb_ref[0]
    floor_lb, one_minus_lb = jnp.maximum(lb, LB_FLOOR), 1.0 - lb

    def emit(rows, acc):
        gated = one_minus_lb * _sigmoid(acc)
        k_ref[rows, :] = (one_minus_lb - gated).astype(k_ref.dtype)
        l2f_ref[rows, :] = jnp.log2(floor_lb + gated).astype(l2f_ref.dtype)
    _project_rows(h_ref, w_ref, wb_ref, emit, 128)


def _proj(kernel_fn, name, h, w, layer, col_of, n_cols, n_out=1, extra=()):
    t = h.shape[0]
    rows = PROJ_BLOCK_ROWS // n_out
    out_spec = pl.BlockSpec((rows, D_MODEL), lambda j, i: (i, j))
    out_shape = jax.ShapeDtypeStruct((t, n_cols * D_MODEL), BF16)
    res = pl.pallas_call(
        kernel_fn,
        grid=(n_cols, t // rows),
        in_specs=[pl.BlockSpec((rows, D_MODEL), lambda j, i: (i, 0)),
                  pl.BlockSpec((None, D_MODEL, D_MODEL), lambda j, i: (layer, 0, col_of(j)))]
                 + [pl.BlockSpec(shape, imap) for _, shape, imap in extra],
        out_specs=[out_spec] * n_out,
        out_shape=[out_shape] * n_out,
        scratch_shapes=[pltpu.VMEM((D_MODEL, D_MODEL), BF16)],
        compiler_params=_compiler_params(("parallel", "arbitrary")),
        name=name,
    )(h, w, *[a for a, _, _ in extra])
    return res if n_out > 1 else res[0]


def _in_proj(h, w, layer, lb_layer, ln_w, ln_b):
    row = lambda a: a.reshape(1, D_MODEL)
    const = lambda j, i: (0, 0)
    scales = jnp.ones((3, 1, D_MODEL), F32).at[0].multiply(HEAD_DIM ** -0.5)
    gates = _proj(_proj_silu_kernel, "proj_silu", h, w, layer, lambda j: (j * 7 + 1) // 2, 3,
                  extra=[(scales, (1, 1, D_MODEL), lambda j, i: (j, 0, 0))])
    k, l2f = _proj(_proj_forget_kernel, "proj_forget", h, w, layer, lambda j: j + COL_FF, 2, n_out=2,
                   extra=[(lb_layer, (1, 1, D_MODEL), lambda j, i: (j, 0, 0))])
    plain = _proj(_proj_plain_kernel, "proj_plain", h, w, layer, lambda j: COL_I + 2 * j, 2)
    v = _proj(_proj_gelu_ln_kernel, "proj_gelu_ln", h, w, layer, lambda j: j + COL_V, 1,
              extra=[(row(ln_w), (1, D_MODEL), const), (row(ln_b), (1, D_MODEL), const)])
    merges = _proj(_proj_sigmoid_kernel, "proj_sigmoid", h, w, layer, lambda j: j + COL_MA, 2)
    return gates, k, l2f, plain, v, merges


def _dot_nt(a, b):
    return lax.dot_general(a, b, (((1,), (1,)), ((), ())), preferred_element_type=F32)


def _dot_tn(a, b):
    return lax.dot_general(a, b, (((0,), (0,)), ((), ())), preferred_element_type=F32)


def _boundary_rows(g, block, row_in_parent):
    c, width = g.shape
    parent = 2 * block
    g3 = g.reshape(c // 8, 8, width)
    sub = lax.broadcasted_iota(jnp.int32, g3.shape, 1)
    r = jnp.broadcast_to(g3[:, row_in_parent:row_in_parent + 1, :], g3.shape)
    for p in range(1, 8 // parent):
        row = p * parent + row_in_parent
        r = jnp.where(sub >= p * parent, jnp.broadcast_to(g3[:, row:row + 1, :], g3.shape), r)
    return r.reshape(c, width)


def _level_operands(q, kf, kb, gf, gb, block):
    c = q.shape[0]
    if block >= 8:
        eq_parts, ek_parts, k_parts = [], [], []
        for blk in range(c // block):
            rows = slice(blk * block, (blk + 1) * block)
            edge = (blk | 1) * block
            if blk % 2:
                eq_parts.append(gf[rows] - gf[edge - 1:edge])
                ek_parts.append(gb[edge:edge + 1] - gb[rows])
                k_parts.append(kb[rows])
            else:
                eq_parts.append(gb[rows] - gb[edge:edge + 1])
                ek_parts.append(gf[edge - 1:edge] - gf[rows])
                k_parts.append(kf[rows])
        q_l = q * jnp.exp2(jnp.concatenate(eq_parts, axis=0))
        k_l = jnp.concatenate(k_parts, axis=0) * jnp.exp2(jnp.concatenate(ek_parts, axis=0))
    else:
        d_f = gf - _boundary_rows(gf, block, block - 1)
        d_b = gb - _boundary_rows(gb, block, block)
        odd = (lax.broadcasted_iota(jnp.int32, q.shape, 0) & block) != 0
        q_l = q * jnp.exp2(jnp.minimum(d_f, d_b))
        k_l = jnp.where(odd, kb, kf) * jnp.exp2(-jnp.maximum(d_f, d_b))
    return q_l.astype(BF16), k_l.astype(BF16)


def _gla_kernel(q_ref, kf_ref, kb_ref, lf_ref, lb_ref, v_ref,
                q2_ref, kb2_ref, lb2_ref, v2_ref,
                o_ref, o2_ref, sf_ref, sb_ref):
    c = REC_CHUNK

    @pl.when(pl.program_id(1) == 0)
    def _():
        sf_ref[...] = jnp.zeros_like(sf_ref)
        sb_ref[...] = jnp.zeros_like(sb_ref)

    row = lax.broadcasted_iota(jnp.int32, (c, c), 0)
    col = lax.broadcasted_iota(jnp.int32, (c, c), 1)
    tri_lower = jnp.where(col <= row, 1.0, 0.0).astype(BF16)
    tri_upper = jnp.where(col >= row, 1.0, 0.0).astype(BF16)
    xor = row ^ col
    heads = [slice(h * HEAD_DIM, (h + 1) * HEAD_DIM) for h in range(N_HEADS)]

    def chunk(ci, carry):
        rows = pl.ds(pl.multiple_of(ci * c, c), c)
        rows2 = pl.ds(pl.multiple_of((GLA_STEP_CHUNKS - 1 - ci) * c, c), c)

        gb2 = jnp.dot(tri_upper, lb2_ref[0, rows2, :], preferred_element_type=F32)
        g_first = gb2[0:1, :]
        qdec2 = (q2_ref[0, rows2, :].astype(F32) * jnp.exp2(gb2)).astype(BF16)
        kdec2 = (kb2_ref[0, rows2, :].astype(F32) * jnp.exp2(g_first - gb2)).astype(BF16)
        decay2 = jnp.exp2(g_first)
        for h, hs in enumerate(heads):
            s_b = sb_ref[h]
            o2_ref[0, rows2, hs] = _dot_nt(qdec2[:, hs], s_b.astype(BF16)).astype(o2_ref.dtype)
            sb_ref[h] = s_b * decay2[:, hs] + _dot_tn(v2_ref[0, rows2, hs], kdec2[:, hs])

        q16 = q_ref[0, rows, :]
        q, kf, kb = q16.astype(F32), kf_ref[0, rows, :].astype(F32), kb_ref[0, rows, :].astype(F32)
        lf16, lb16 = lf_ref[0, rows, :], lb_ref[0, rows, :]
        gf = jnp.dot(tri_lower, lf16, preferred_element_type=F32)
        gb = jnp.dot(tri_upper, lb16, preferred_element_type=F32)

        k_d = (kf + kb).astype(BF16)
        attn = [_dot_nt(q16[:, hs], k_d[:, hs]) for hs in heads]
        odd = (lax.broadcasted_iota(jnp.int32, q.shape, 0) & 1) != 0
        step = jnp.where(odd, lf16.astype(F32), lb16.astype(F32))
        q_l = (q * jnp.exp2(step)).astype(BF16)
        k_l = jnp.where(odd, kb, kf).astype(BF16)
        for h, hs in enumerate(heads):
            attn[h] = jnp.where(xor >= 1, _dot_nt(q_l[:, hs], k_l[:, hs]), attn[h])
        for block in reversed(LEVELS):
            q_l, k_l = _level_operands(q, kf, kb, gf, gb, block)
            for h, hs in enumerate(heads):
                attn[h] = jnp.where(xor >= block, _dot_nt(q_l[:, hs], k_l[:, hs]), attn[h])

        g_last = gf[c - 1:c, :]
        qdec = (q * jnp.exp2(gf)).astype(BF16)
        kdec = (kf * jnp.exp2(g_last - gf)).astype(BF16)
        decay = jnp.exp2(g_last)
        for h, hs in enumerate(heads):
            s_f = sf_ref[h]
            v = v_ref[0, rows, hs]
            o = jnp.dot(attn[h].astype(BF16), v, preferred_element_type=F32)
            o_ref[0, rows, hs] = (o + _dot_nt(qdec[:, hs], s_f.astype(BF16))).astype(o_ref.dtype)
            sf_ref[h] = s_f * decay[:, hs] + _dot_tn(v, kdec[:, hs])
        return carry

    lax.fori_loop(0, GLA_STEP_CHUNKS, chunk, 0, unroll=4)


def _gla(gates, k, l2f, val, batch, seq):
    rows = GLA_STEP_CHUNKS * REC_CHUNK
    n = seq // rows
    as3 = lambda a: a.reshape(batch, seq, a.shape[-1])

    def fwd(colblock):
        return pl.BlockSpec((1, rows, D_MODEL), lambda bi, ni: (bi, ni, colblock))

    def bwd(colblock):
        return pl.BlockSpec((1, rows, D_MODEL), lambda bi, ni: (bi, n - 1 - ni, colblock))

    out_shape = jax.ShapeDtypeStruct((batch, seq, D_MODEL), BF16)
    g3, k3, l3, v3 = as3(gates), as3(k), as3(l2f), as3(val)
    return pl.pallas_call(
        _gla_kernel,
        grid=(batch, n),
        in_specs=[fwd(0), fwd(0), fwd(1), fwd(0), fwd(1), fwd(0),
                  bwd(0), bwd(1), bwd(1), bwd(0)],
        out_specs=[fwd(0), bwd(0)],
        out_shape=[out_shape, out_shape],
        scratch_shapes=[pltpu.VMEM((N_HEADS, HEAD_DIM, HEAD_DIM), F32),
                        pltpu.VMEM((N_HEADS, HEAD_DIM, HEAD_DIM), F32)],
        compiler_params=_compiler_params(("parallel", "arbitrary")),
        name="gla",
    )(g3, k3, k3, l3, l3, v3,
      g3, k3, l3, v3)


def _merge_kernel(o1_ref, o2_ref, ga_ref, u_ref, v_ref, gb_ref, ma_ref, mb_ref, x_ref,
                  gnw_ref, ws_ref, bs_ref, wa_ref, wb_ref, wo_ref, nw_ref,
                  xo_ref, ho_ref):
    rows = x_ref.shape[0]

    o = o1_ref[...].astype(F32) + o2_ref[...].astype(F32)
    parts = []
    for h in range(N_HEADS):
        hs = slice(h * HEAD_DIM, (h + 1) * HEAD_DIM)
        oh = o[:, hs]
        oh = oh * lax.rsqrt(jnp.mean(oh * oh, axis=-1, keepdims=True) + RMS_EPS) * gnw_ref[...]
        parts.append((oh * ga_ref[:, hs].astype(F32)).astype(BF16))
    y_a = jnp.concatenate(parts, axis=1)

    row_parts = []
    for ci in range(rows // SG_CHUNK):
        rs = slice(ci * SG_CHUNK, (ci + 1) * SG_CHUNK)
        parts = []
        for g in range(SG_GROUPS):
            hs = slice(g * HEAD_DIM, (g + 1) * HEAD_DIM)
            mixed = jnp.dot(ws_ref[g], v_ref[rs, hs], preferred_element_type=F32) + bs_ref[g]
            parts.append((_gelu(u_ref[rs, hs].astype(F32)) * mixed * gb_ref[rs, hs].astype(F32)).astype(BF16))
        row_parts.append(jnp.concatenate(parts, axis=1))
    y_b = jnp.concatenate(row_parts, axis=0)

    merged = (ma_ref[...].astype(F32) * jnp.dot(y_a, wa_ref[...], preferred_element_type=F32)
              + mb_ref[...].astype(F32) * jnp.dot(y_b, wb_ref[...], preferred_element_type=F32))
    x_new = x_ref[...] + jnp.dot(merged.astype(BF16), wo_ref[...], preferred_element_type=F32)
    xo_ref[...] = x_new
    y = x_new * lax.rsqrt(jnp.mean(x_new * x_new, axis=-1, keepdims=True) + RMS_EPS)
    ho_ref[...] = (y * nw_ref[...]).astype(ho_ref.dtype)


def _merge(o1, o2, gates, plain, v, merges, x, gnorm_w, w_s, b_s, w_a, w_b, w_o, next_norm_w, h_dtype, block_rows=512):
    t = x.shape[0]

    def tok(colblock=0):
        return pl.BlockSpec((block_rows, D_MODEL), lambda i: (i, colblock))

    def whole(shape):
        return pl.BlockSpec(shape, lambda i: (0,) * len(shape), pipeline_mode=pl.Buffered(1))

    bs_full = jnp.broadcast_to(b_s.astype(F32)[:, :, None], (SG_GROUPS, SG_CHUNK, HEAD_DIM))
    return pl.pallas_call(
        _merge_kernel,
        grid=(t // block_rows,),
        in_specs=[tok(), tok(), tok(1), tok(1), tok(), tok(2), tok(0), tok(1), tok(),
                  whole((1, HEAD_DIM)),
                  whole((SG_GROUPS, SG_CHUNK, SG_CHUNK)), whole((SG_GROUPS, SG_CHUNK, HEAD_DIM)),
                  whole((D_MODEL, D_MODEL)), whole((D_MODEL, D_MODEL)), whole((D_MODEL, D_MODEL)),
                  whole((1, D_MODEL))],
        out_specs=[tok(), tok()],
        out_shape=[jax.ShapeDtypeStruct((t, D_MODEL), F32), jax.ShapeDtypeStruct((t, D_MODEL), h_dtype)],
        compiler_params=_compiler_params(("parallel",)),
        name="merge",
    )(o1, o2, gates, plain, v, gates, merges, merges, x,
      gnorm_w.reshape(1, HEAD_DIM), w_s.astype(BF16), bs_full, w_a, w_b, w_o, next_norm_w.reshape(1, D_MODEL))


def kernel(x, norm_w, w_in, lower_bounds, gnorm_w, ln_w, ln_b, w_s, b_s, w_proj_a, w_proj_b, w_out, final_norm_w):
    batch, seq, _ = x.shape
    t = batch * seq
    lb = _lower_bounds(lower_bounds)
    x2 = x.reshape(t, D_MODEL)
    h = _rmsnorm(x2, norm_w[0], BF16)
    for layer in range(DEPTH):
        last = layer == DEPTH - 1
        gates, k, l2f, plain, v, merges = _in_proj(h, w_in, layer, lb[layer], ln_w[layer], ln_b[layer])
        o1, o2 = _gla(gates, k, l2f, plain, batch, seq)
        x2, h = _merge(o1.reshape(t, D_MODEL), o2.reshape(t, D_MODEL), gates, plain, v, merges, x2,
                       gnorm_w[layer], w_s[layer], b_s[layer],
                       w_proj_a[layer].astype(BF16), w_proj_b[layer].astype(BF16), w_out[layer].astype(BF16),
                       final_norm_w if last else norm_w[layer + 1], F32 if last else BF16)
    return h.reshape(batch, seq, D_MODEL)
```
